```python
import jax, jax.numpy as jnp
from jax import lax
import numpy as np

D_MODEL = 1024
BATCH = 16
SEQ = 2048
DEPTH = 4

CTX_LEN = 256
GRID_W = 64
N_MOD = 9
EPS = 1e-6
FFN_RESIDUAL = 0.5
D_FF = 2816
HEAD_DIM = 64
ROPE_PAIRS = HEAD_DIM // 4
ROPE_THETA = 10000.0
BLOCK = 128
A_HEADS = 8
A_KV = 2
WINDOW = 128
B_HEADS = 8
B_KV = 2
C_HEADS = 4
C_HEAD_DIM = 128
C_CONV = 5
MLSTM_CHUNK = 64
F_BIAS_LO = 3.0
F_BIAS_HI = 6.0
A_WIDTH = A_HEADS * HEAD_DIM
A_KV_WIDTH = A_KV * HEAD_DIM
B_WIDTH = B_HEADS * HEAD_DIM
B_KV_WIDTH = B_KV * HEAD_DIM
C_WIDTH = C_HEADS * C_HEAD_DIM
BRANCH_WIDTH = 512
N_BRANCH = 3
IN_SIZES = (A_WIDTH, A_KV_WIDTH, A_KV_WIDTH, B_WIDTH, B_KV_WIDTH, B_KV_WIDTH, C_WIDTH, C_WIDTH, C_WIDTH, C_WIDTH, 4 * C_HEADS, N_BRANCH * D_MODEL)
IN_WIDTH = sum(IN_SIZES)

kernel_name = 'hybrid_diffusion_trunk'


def rms_norm(x, g):
    xf = x.astype(jnp.float32)
    y = xf * lax.rsqrt(jnp.mean(xf * xf, axis=-1, keepdims=True) + EPS)
    return (y * g.astype(jnp.float32)).astype(x.dtype)


def swiglu(x, w_gate, w_up, w_down):
    return (jax.nn.silu(x @ w_gate) * (x @ w_up)) @ w_down


def modulated_norm(x, mod, k, norm_g):
    return rms_norm(x, norm_g[2 * k]) * (1 + mod[:, :, 3 * k + 1]) + mod[:, :, 3 * k]


def gated_residual(x, y, mod, k, norm_g, weight):
    return x + weight * mod[:, :, 3 * k + 2] * rms_norm(y, norm_g[2 * k + 1])


def half_ffn(x, mod, k, norm_g, w):
    h = modulated_norm(x, mod, k, norm_g)
    return gated_residual(x, swiglu(h, *w), mod, k, norm_g, FFN_RESIDUAL)


def split_columns(p):
    outs, start = [], 0
    for size in IN_SIZES:
        outs.append(p[..., start:start + size])
        start += size
    return outs


def heads(t, n):
    return t.reshape(t.shape[:-1] + (n, -1))


def group(t, kv):
    return t.reshape(t.shape[:2] + (kv, -1, t.shape[-1]))


def axial_rope_tables(rows):
    row = jnp.repeat(jnp.arange(rows), GRID_W)
    col = jnp.tile(jnp.arange(GRID_W), rows)
    freqs = ROPE_THETA ** (-jnp.arange(ROPE_PAIRS, dtype=jnp.float32) / ROPE_PAIRS)
    ang = jnp.stack([row[:, None] * freqs, col[:, None] * freqs], axis=1)
    return jnp.cos(ang), jnp.sin(ang)


def apply_rope(x, cos, sin):
    xs = x.astype(jnp.float32).reshape(x.shape[:-1] + (2, 2, ROPE_PAIRS))
    x1, x2 = xs[..., 0, :], xs[..., 1, :]
    c, s = cos[None, :, None], sin[None, :, None]
    out = jnp.stack([x1 * c - x2 * s, x2 * c + x1 * s], axis=-2)
    return out.reshape(x.shape).astype(x.dtype)


def gqa_softmax(q, k, v, mask, sink):
    kv, g = q.shape[2], q.shape[3]
    s = jnp.einsum('bqkgd,bskd->bkgqs', q, k).astype(jnp.float32) * (q.shape[-1] ** -0.5)
    if mask is not None:
        s = jnp.where(mask, s, -jnp.inf)
    if sink is not None:
        sink_col = jnp.broadcast_to(sink.astype(jnp.float32).reshape(1, kv, g, 1, 1), s.shape[:-1] + (1,))
        p = jax.nn.softmax(jnp.concatenate([s, sink_col], axis=-1), axis=-1)[..., :-1]
    else:
        p = jax.nn.softmax(s, axis=-1)
    o = jnp.einsum('bkgqs,bskd->bqkgd', p.astype(v.dtype), v)
    return o.reshape(o.shape[:2] + (-1,))


def blockwise_queries(fn, q):
    bsz, t_lat = q.shape[:2]
    nb = t_lat // BLOCK
    qb = jnp.swapaxes(q.reshape((bsz, nb, BLOCK) + q.shape[2:]), 0, 1)
    out = lax.map(fn, (qb, jnp.arange(nb)))
    return jnp.swapaxes(out, 0, 1).reshape(bsz, t_lat, -1)


def windowed_attention(q, k, v, k_ctx, v_ctx, sink):
    t_lat, t_ctx = q.shape[1], k_ctx.shape[1]
    span = BLOCK + 2 * WINDOW
    pad = ((0, 0), (WINDOW, WINDOW), (0, 0), (0, 0))
    kp, vp = jnp.pad(k, pad), jnp.pad(v, pad)
    r, j = jnp.arange(BLOCK), jnp.arange(span)
    near = jnp.abs(r[:, None] + WINDOW - j[None, :]) <= WINDOW
    ctx_cols = jnp.ones((BLOCK, t_ctx), bool)

    def one_block(args):
        qi, i = args
        start = i * BLOCK
        kw = lax.dynamic_slice_in_dim(kp, start, span, axis=1)
        vw = lax.dynamic_slice_in_dim(vp, start, span, axis=1)
        s_pos = start - WINDOW + j
        in_range = (s_pos >= 0) & (s_pos < t_lat)
        mask = jnp.concatenate([ctx_cols, near & in_range[None, :]], axis=1)
        return gqa_softmax(qi, jnp.concatenate([k_ctx, kw], axis=1), jnp.concatenate([v_ctx, vw], axis=1), mask, sink)

    return blockwise_queries(one_block, q)


def dense_attention(q, k_all, v_all):
    return blockwise_queries(lambda args: gqa_softmax(args[0], k_all, v_all, None, None), q)


def centred_conv(x, w, b):
    y = lax.conv_general_dilated(x, w[:, None, :].astype(x.dtype), window_strides=(1,),
                                 padding=[(C_CONV // 2, C_CONV // 2)],
                                 dimension_numbers=('NWC', 'WIO', 'NWC'), feature_group_count=x.shape[-1])
    return y + b


def mlstm_scan(q, k, v, li, lf, state):
    bsz, nh, t_len, dk = q.shape
    nc = t_len // MLSTM_CHUNK

    def chunks(t):
        t = t.astype(jnp.float32).reshape((bsz, nh, nc, MLSTM_CHUNK) + t.shape[3:])
        return jnp.moveaxis(t, 2, 0)

    causal = jnp.tril(jnp.ones((MLSTM_CHUNK, MLSTM_CHUNK), bool))

    def step(carry, inp):
        C, n, m = carry
        qc, kc, vc, ic, fc = inp
        b = jnp.cumsum(fc, axis=-1)
        d_log = jnp.where(causal, b[..., :, None] - b[..., None, :] + ic[..., None, :], -jnp.inf)
        inter = b + m[..., None]
        m_t = jnp.maximum(inter, jnp.max(d_log, axis=-1))
        w_intra = jnp.exp(d_log - m_t[..., None])
        w_inter = jnp.exp(inter - m_t)
        s = jnp.einsum('bhtd,bhsd->bhts', qc, kc) * w_intra
        num = jnp.einsum('bhts,bhsv->bhtv', s, vc) + w_inter[..., None] * jnp.einsum('bhtd,bhdv->bhtv', qc, C)
        den = jnp.sum(s, axis=-1) + w_inter * jnp.einsum('bhtd,bhd->bht', qc, n)
        h = num / jnp.maximum(jnp.abs(den), jnp.exp(-m_t))[..., None]
        m_new = m_t[..., -1]
        w_end = jnp.exp(b[..., -1:] - b + ic - m_new[..., None])
        decay = jnp.exp(b[..., -1] + m - m_new)
        C = decay[..., None, None] * C + jnp.einsum('bhs,bhsd,bhsv->bhdv', w_end, kc, vc)
        n = decay[..., None] * n + jnp.einsum('bhs,bhsd->bhd', w_end, kc)
        return (C, n, m_new), h

    state, h = lax.scan(step, state, (chunks(q) * (dk ** -0.5), chunks(k), chunks(v), chunks(li), chunks(lf)))
    h = jnp.moveaxis(h, 0, 2).reshape(bsz, nh, t_len, -1)
    return h.astype(v.dtype), state


def mlstm_prep(q, k, v, g, conv_w, conv_b, gate_b):
    qk = jax.nn.silu(centred_conv(jnp.concatenate([q, k], axis=-1), conv_w, conv_b))
    q, k = jnp.split(qk, 2, axis=-1)
    q, k, v = (jnp.transpose(heads(t, C_HEADS), (0, 2, 1, 3)) for t in (q, k, v))
    gp = jnp.transpose((g + gate_b).astype(jnp.float32), (0, 2, 1))
    i_f, f_f, i_b, f_b = jnp.split(gp, 4, axis=1)
    fwd = (q, k, v, i_f, jax.nn.log_sigmoid(f_f))
    bwd = tuple(jnp.flip(t, axis=2) for t in (q, k, v, i_b, jax.nn.log_sigmoid(f_b)))
    return fwd, bwd


def mlstm_mixer(lat, ctx_in, need_ctx, conv_w, conv_b, gate_b, head_g):
    q, k, v, o, g = lat
    qc, kc, vc, oc, gc = ctx_in
    fwd, bwd = mlstm_prep(q, k, v, g, conv_w, conv_b, gate_b)
    fwd_c, bwd_c = mlstm_prep(qc, kc, vc, gc, conv_w, conv_b, gate_b)
    bsz = q.shape[0]
    zero = (jnp.zeros((bsz, C_HEADS, C_HEAD_DIM, C_HEAD_DIM), jnp.float32),
            jnp.zeros((bsz, C_HEADS, C_HEAD_DIM), jnp.float32),
            jnp.zeros((bsz, C_HEADS), jnp.float32))
    h_cf, st_f = mlstm_scan(*fwd_c, zero)
    h_cb, st_b = mlstm_scan(*bwd_c, zero)
    h_f, _ = mlstm_scan(*fwd, st_f)
    h_b, _ = mlstm_scan(*bwd, st_b)

    def readout(h, og):
        h = rms_norm(jnp.transpose(h, (0, 2, 1, 3)), head_g.reshape(C_HEADS, C_HEAD_DIM))
        return jax.nn.sigmoid(og) * h.reshape(h.shape[:2] + (C_WIDTH,))

    y = readout(h_f + jnp.flip(h_b, axis=2), o)
    if not need_ctx:
        return y, None
    return y, readout(h_cf + jnp.flip(h_cb, axis=2), oc)


def merge(y_a, y_b, y_m, gate_logits, w_branch, w_out):
    y = jnp.stack([y_a, y_b, y_m], axis=-2)
    z = jnp.einsum('btnw,nwd->btnd', y, w_branch)
    g = jax.nn.sigmoid(gate_logits.reshape(gate_logits.shape[:-1] + (N_BRANCH, D_MODEL)))
    return jnp.sum(g * z, axis=-2) @ w_out


def token_mixers(h, hc, need_ctx, cos, sin, w_in, attn_sink, qk_norm_g, conv_w, conv_b, gate_b, head_g, w_branch, w_out):
    qa, ka, va, qb, kb, vb, qm, km, vm, om, gm, gl = split_columns(h @ w_in)
    qa_c, ka_c, va_c, qb_c, kb_c, vb_c, qm_c, km_c, vm_c, om_c, gm_c, gl_c = split_columns(hc @ w_in)
    qa = group(apply_rope(heads(qa, A_HEADS), cos, sin), A_KV)
    ka = apply_rope(heads(ka, A_KV), cos, sin)
    va = heads(va, A_KV)
    qa_c, ka_c, va_c = group(heads(qa_c, A_HEADS), A_KV), heads(ka_c, A_KV), heads(va_c, A_KV)
    y_a = windowed_attention(qa, ka, va, ka_c, va_c, attn_sink)
    qn, kn = qk_norm_g[0], qk_norm_g[1]
    qb = group(apply_rope(rms_norm(heads(qb, B_HEADS), qn), cos, sin), B_KV)
    kb = apply_rope(rms_norm(heads(kb, B_KV), kn), cos, sin)
    vb = heads(vb, B_KV)
    qb_c = group(rms_norm(heads(qb_c, B_HEADS), qn), B_KV)
    kb_c, vb_c = rms_norm(heads(kb_c, B_KV), kn), heads(vb_c, B_KV)
    y_b = dense_attention(qb, jnp.concatenate([kb_c, kb], axis=1), jnp.concatenate([vb_c, vb], axis=1))
    y_m, y_m_c = mlstm_mixer((qm, km, vm, om, gm), (qm_c, km_c, vm_c, om_c, gm_c), need_ctx,
                             conv_w, conv_b, gate_b, head_g)
    y = merge(y_a, y_b, y_m, gl, w_branch, w_out)
    if not need_ctx:
        return y, None
    y_a_c = gqa_softmax(qa_c, ka_c, va_c, None, attn_sink)
    y_b_c = gqa_softmax(qb_c, kb_c, vb_c, None, None)
    return y, merge(y_a_c, y_b_c, y_m_c, gl_c, w_branch, w_out)


def setup_inputs(seed: int = 0) -> dict:
    key = jax.random.key(seed)
    ks = jax.random.split(key, 20)
    f32 = jnp.float32

    def nrm(k, shape, scale):
        return jax.random.normal(k, shape, f32) * scale

    L = DEPTH
    i_bias = nrm(ks[15], (L, 2, C_HEADS), 0.1)
    f_bias = jnp.linspace(F_BIAS_LO, F_BIAS_HI, C_HEADS, dtype=f32) + nrm(ks[16], (L, 2, C_HEADS), 0.1)
    return {
        'x': nrm(ks[0], (BATCH, SEQ, D_MODEL), 1.0),
        'c': nrm(ks[1], (BATCH, D_MODEL), 1.0),
        'ctx': nrm(ks[2], (BATCH, CTX_LEN, D_MODEL), 1.0),
        'c_ctx': nrm(ks[3], (D_MODEL,), 1.0),
        'w_ada': nrm(ks[4], (L, D_MODEL, N_MOD * D_MODEL), 0.5 * D_MODEL ** -0.5),
        'b_ada': nrm(ks[5], (L, N_MOD * D_MODEL), 0.01),
        'norm_g': 1.0 + nrm(ks[6], (L, 6, D_MODEL), 0.02),
        'ffn_w_gate': nrm(ks[7], (L, 2, D_MODEL, D_FF), D_MODEL ** -0.5),
        'ffn_w_up': nrm(ks[8], (L, 2, D_MODEL, D_FF), D_MODEL ** -0.5),
        'ffn_w_down': nrm(ks[9], (L, 2, D_FF, D_MODEL), D_FF ** -0.5),
        'w_in': nrm(ks[10], (L, D_MODEL, IN_WIDTH), D_MODEL ** -0.5),
        'attn_sink': nrm(ks[11], (L, A_HEADS), 0.5),
        'qk_norm_g': 1.0 + nrm(ks[12], (L, 2, HEAD_DIM), 0.02),
        'conv_w': nrm(ks[13], (L, C_CONV, 2 * C_WIDTH), C_CONV ** -0.5),
        'conv_b': nrm(ks[14], (L, 2 * C_WIDTH), 0.01),
        'mlstm_gate_b': jnp.stack([i_bias, f_bias], axis=2).reshape(L, 4 * C_HEADS),
        'mlstm_norm_g': 1.0 + nrm(ks[17], (L, C_WIDTH), 0.02),
        'w_branch': nrm(ks[18], (L, N_BRANCH, BRANCH_WIDTH, D_MODEL), BRANCH_WIDTH ** -0.5),
        'w_out': nrm(ks[19], (L, D_MODEL, D_MODEL), D_MODEL ** -0.5),
    }


def reference(x, c, ctx, c_ctx, w_ada, b_ada, norm_g, ffn_w_gate, ffn_w_up, ffn_w_down, w_in, attn_sink,
              qk_norm_g, conv_w, conv_b, mlstm_gate_b, mlstm_norm_g, w_branch, w_out):
    bsz, t_lat = x.shape[:2]
    rows = t_lat // GRID_W
    cos, sin = axial_rope_tables(rows)
    for l in range(DEPTH):
        last = l == DEPTH - 1
        mod = (jax.nn.silu(c) @ w_ada[l] + b_ada[l]).reshape(bsz, 1, N_MOD, D_MODEL)
        mod_c = (jax.nn.silu(c_ctx) @ w_ada[l] + b_ada[l]).reshape(1, 1, N_MOD, D_MODEL)
        ffn1 = (ffn_w_gate[l, 0], ffn_w_up[l, 0], ffn_w_down[l, 0])
        ffn2 = (ffn_w_gate[l, 1], ffn_w_up[l, 1], ffn_w_down[l, 1])
        x = half_ffn(x, mod, 0, norm_g[l], ffn1)
        ctx = half_ffn(ctx, mod_c, 0, norm_g[l], ffn1)
        h = modulated_norm(x, mod, 1, norm_g[l])
        hc = modulated_norm(ctx, mod_c, 1, norm_g[l])
        y, y_c = token_mixers(h, hc, not last, cos, sin, w_in[l], attn_sink[l], qk_norm_g[l], conv_w[l], conv_b[l],
                              mlstm_gate_b[l], mlstm_norm_g[l], w_branch[l], w_out[l])
        x = gated_residual(x, y, mod, 1, norm_g[l], 1.0)
        x = half_ffn(x, mod, 2, norm_g[l], ffn2)
        if not last:
            ctx = gated_residual(ctx, y_c, mod_c, 1, norm_g[l], 1.0)
            ctx = half_ffn(ctx, mod_c, 2, norm_g[l], ffn2)
    return x
```

```python
import functools

import jax
import jax.numpy as jnp
from jax import lax
from jax.experimental import pallas as pl
from jax.experimental.pallas import tpu as pltpu

F32 = jnp.float32
BF16 = jnp.bfloat16

EPS = 1e-6
N_MOD = 9
GRID_W = 64
ROPE_THETA = 10000.0
HEAD_DIM = 64
ROPE_PAIRS = HEAD_DIM // 4
N_HEADS = 8
N_KV = 2
ATT_WIDTH = N_HEADS * HEAD_DIM
WINDOW = 128
C_HEADS = 4
C_HEAD_DIM = 128
C_WIDTH = C_HEADS * C_HEAD_DIM
C_CONV = 5
CHUNK = 64
N_GATES = 4 * C_HEADS
N_BRANCH = 3

LANES = 128
SUBLANES = 8
VMEM_LIMIT_BYTES = 56 * 1024 * 1024


def _cparams(n_axes):
    return pltpu.CompilerParams(
        dimension_semantics=("arbitrary",) * n_axes,
        vmem_limit_bytes=VMEM_LIMIT_BYTES,
    )


def _pick_tile(n, candidates=(512, 256, 128)):
    for c in candidates:
        if n % c == 0:
            return c
    raise ValueError(f"no tile size for {n}")


def _resident(shape):
    zeros = (0,) * len(shape)
    return pl.BlockSpec(shape, lambda *_: zeros, pipeline_mode=pl.Buffered(1))


def _mm(a, b):
    return jnp.dot(a, b, preferred_element_type=F32)


def _mm_nt(a, b):
    return lax.dot_general(a, b, (((1,), (1,)), ((), ())), preferred_element_type=F32)


def _rms(x, g):
    return x * lax.rsqrt(jnp.mean(x * x, axis=-1, keepdims=True) + EPS) * g


def _sigmoid(x):
    return 1.0 / (1.0 + jnp.exp(-x))


def _silu(x):
    return x * _sigmoid(x)


def _log_sigmoid(x):
    return jnp.minimum(x, 0.0) - jnp.log(1.0 + jnp.exp(-jnp.abs(x)))


def _mod_norm(x, mod, ng, k):
    return _rms(x, ng[2 * k:2 * k + 1]) * (1.0 + mod[3 * k + 1:3 * k + 2]) + mod[3 * k:3 * k + 1]


def _gated_residual(x, y, mod, ng, k, weight):
    return x + (weight * mod[3 * k + 2:3 * k + 3]) * _rms(y, ng[2 * k + 1:2 * k + 2])


def _mod_kernel(c_ref, w_ref, b_ref, o_ref):
    a = _silu(c_ref[...]).astype(BF16)
    o_ref[0] = _mm(a, w_ref[0].astype(BF16)) + b_ref[0]


def _modulation(c_all, w_ada, b_ada):
    n_layers, d, nd = w_ada.shape
    r = c_all.shape[0]
    out = pl.pallas_call(
        _mod_kernel,
        grid=(n_layers, nd // d),
        in_specs=[
            pl.BlockSpec((r, d), lambda l, j: (0, 0)),
            pl.BlockSpec((1, d, d), lambda l, j: (l, 0, j)),
            pl.BlockSpec((1, 1, d), lambda l, j: (l, 0, j)),
        ],
        out_specs=pl.BlockSpec((1, r, d), lambda l, j: (l, 0, j)),
        out_shape=jax.ShapeDtypeStruct((n_layers, r, nd), F32),
        compiler_params=_cparams(2),
        name="adaln_mod",
    )(c_all, w_ada, b_ada.reshape(n_layers, 1, nd))
    return out.reshape(n_layers, r, N_MOD, d)


FFN_COL_CHUNK = 256


def _ffn_kernel(x_ref, mod_ref, ng_ref, wg_ref, wu_ref, wd_ref, o_ref, a_scr, *, k):
    x = x_ref[...]
    mod = mod_ref[0]
    ng = ng_ref[...]
    h = _mod_norm(x, mod, ng, k).astype(BF16)
    d_ff = wg_ref.shape[1]
    for j in range(d_ff // FFN_COL_CHUNK):
        cols = slice(j * FFN_COL_CHUNK, (j + 1) * FFN_COL_CHUNK)
        g = _mm(h, wg_ref[:, cols])
        u = _mm(h, wu_ref[:, cols])
        a_scr[:, cols] = (_silu(g) * u).astype(BF16)
    y = _mm(a_scr[...], wd_ref[...])
    o_ref[...] = _gated_residual(x, y, mod, ng, k, 0.5)


def _half_ffn(x, mod, ng, wg, wu, wd, k, rows_per_mod, mod_base):
    m, d = x.shape
    d_ff = wg.shape[1]
    tm = _pick_tile(rows_per_mod)
    tiles_per_mod = rows_per_mod // tm
    return pl.pallas_call(
        functools.partial(_ffn_kernel, k=k),
        grid=(m // tm,),
        in_specs=[
            pl.BlockSpec((tm, d), lambda i: (i, 0)),
            pl.BlockSpec((1, N_MOD, d), lambda i: (mod_base + i // tiles_per_mod, 0, 0)),
            _resident(ng.shape),
            _resident(wg.shape),
            _resident(wu.shape),
            _resident(wd.shape),
        ],
        out_specs=pl.BlockSpec((tm, d), lambda i: (i, 0)),
        out_shape=jax.ShapeDtypeStruct((m, d), F32),
        scratch_shapes=[pltpu.VMEM((tm, d_ff), BF16)],
        compiler_params=_cparams(1),
        name=f"half_ffn{k}",
    )(x, mod, ng, wg, wu, wd)


def _head_rms(x, gain):
    low = lax.broadcasted_iota(jnp.int32, x.shape, 1) < HEAD_DIM
    ss = x * x
    s_lo = jnp.sum(jnp.where(low, ss, 0.0), axis=-1, keepdims=True)
    s_hi = jnp.sum(jnp.where(low, 0.0, ss), axis=-1, keepdims=True)
    inv = jnp.where(low, lax.rsqrt(s_lo * (1.0 / HEAD_DIM) + EPS), lax.rsqrt(s_hi * (1.0 / HEAD_DIM) + EPS))
    return x * inv * gain


def _rope(x, cos, sin_signed):
    lane = lax.broadcasted_iota(jnp.int32, x.shape, 1)
    first_half = (lane & ROPE_PAIRS) == 0
    partner = jnp.where(first_half, pltpu.roll(x, LANES - ROPE_PAIRS, 1), pltpu.roll(x, ROPE_PAIRS, 1))
    return x * cos + partner * sin_signed


def _inproj_kernel(*refs, rope):
    if rope:
        x_ref, mod_ref, ng_ref, cos_ref, sin_ref = refs[:5]
        refs = refs[5:]
        cos, sin = cos_ref[...], sin_ref[...]
    else:
        x_ref, mod_ref, ng_ref = refs[:3]
        refs = refs[3:]
    (qkn_ref, w_qa, w_kva, w_qb, w_kvb, w_qkm, w_vm, w_om, w_g, w_gt,
     qa_ref, ka_ref, va_ref, qb_ref, kb_ref, vb_ref, qkm_ref, vm_ref, om_ref, g_ref, gt_ref) = refs
    h = _mod_norm(x_ref[...], mod_ref[0], ng_ref[...], 1).astype(BF16)
    qn, kn = qkn_ref[0:1], qkn_ref[1:2]
    kv_w = 2 * N_KV * HEAD_DIM

    def blocks(w_ref, first, count, norm_gain, scale):
        outs = []
        for j in range(first, first + count):
            blk = _mm(h, w_ref[:, j * LANES:(j + 1) * LANES])
            if norm_gain is not None:
                blk = _head_rms(blk, norm_gain)
            if rope:
                blk = _rope(blk, cos, sin)
            if scale != 1.0:
                blk = blk * scale
            outs.append(blk.astype(BF16))
        return outs

    q_scale = HEAD_DIM ** -0.5
    for j, blk in enumerate(blocks(w_qa, 0, ATT_WIDTH // LANES, None, q_scale)):
        qa_ref[:, j * LANES:(j + 1) * LANES] = blk
    for j, blk in enumerate(blocks(w_kva, 0, kv_w // LANES, None, 1.0)):
        ka_ref[:, j * LANES:(j + 1) * LANES] = blk
    va_ref[...] = _mm(h, w_kva[:, kv_w:]).astype(BF16)
    for j, blk in enumerate(blocks(w_qb, 0, ATT_WIDTH // LANES, qn, q_scale)):
        qb_ref[:, j * LANES:(j + 1) * LANES] = blk
    for j, blk in enumerate(blocks(w_kvb, 0, kv_w // LANES, kn, 1.0)):
        kb_ref[:, j * LANES:(j + 1) * LANES] = blk
    vb_ref[...] = _mm(h, w_kvb[:, kv_w:]).astype(BF16)
    qkm_ref[...] = _mm(h, w_qkm[...])
    vm_ref[...] = _mm(h, w_vm[...]).astype(BF16)
    om_ref[...] = _mm(h, w_om[...])
    g_ref[...] = _mm(h, w_g[...])
    gt_ref[...] = _mm_nt(w_gt[...], h)


def _in_projection(x, mod, ng, rope_tabs, qkn, weights, rows_per_mod, mod_base, t_seq):
    m, d = x.shape
    tm = _pick_tile(rows_per_mod)
    tiles_per_mod = rows_per_mod // tm
    rope = rope_tabs is not None
    row = lambda i: (i, 0)
    in_specs = [
        pl.BlockSpec((tm, d), row),
        pl.BlockSpec((1, N_MOD, d), lambda i: (mod_base + i // tiles_per_mod, 0, 0)),
        _resident(ng.shape),
    ]
    args = [x, mod, ng]
    if rope:
        tiles_per_seq = t_seq // tm
        in_specs += [pl.BlockSpec((tm, LANES), lambda i: (i % tiles_per_seq, 0))] * 2
        args += list(rope_tabs)
    in_specs += [_resident(qkn.shape)] + [_resident(w.shape) for w in weights]
    args += [qkn] + list(weights)
    kv_w = 2 * N_KV * HEAD_DIM
    widths = [(ATT_WIDTH, BF16), (kv_w, BF16), (kv_w, BF16), (ATT_WIDTH, BF16), (kv_w, BF16), (kv_w, BF16),
              (2 * C_WIDTH, F32), (C_WIDTH, BF16), (C_WIDTH, F32), (N_GATES, F32)]
    out_specs = [pl.BlockSpec((tm, w), row) for w, _ in widths] + [pl.BlockSpec((N_GATES, tm), lambda i: (0, i))]
    out_shape = [jax.ShapeDtypeStruct((m, w), dt) for w, dt in widths] + [jax.ShapeDtypeStruct((N_GATES, m), F32)]
    return pl.pallas_call(
        functools.partial(_inproj_kernel, rope=rope),
        grid=(m // tm,),
        in_specs=in_specs,
        out_specs=out_specs,
        out_shape=out_shape,
        compiler_params=_cparams(1),
        name="in_proj_rope" if rope else "in_proj",
    )(*args)


def _attend(q_ref, k, v, mask, sink_ref, o_ref):
    qb = q_ref.shape[0]
    upper = lax.broadcasted_iota(jnp.int32, (qb, LANES), 1) >= HEAD_DIM
    outs = []
    for hd in range(N_HEADS):
        grp = hd // (N_HEADS // N_KV)
        qblk = q_ref[:, (hd // 2) * LANES:(hd // 2 + 1) * LANES]
        mine = upper if hd % 2 else jnp.logical_not(upper)
        qh = jnp.where(mine, qblk, jnp.zeros_like(qblk))
        s = _mm_nt(qh, k[:, grp * LANES:(grp + 1) * LANES])
        if mask is not None:
            s = jnp.where(mask, s, -jnp.inf)
        mx = jnp.max(s, axis=-1, keepdims=True)
        if sink_ref is not None:
            sink = sink_ref[0, hd]
            mx = jnp.maximum(mx, sink)
        p = jnp.exp(s - mx)
        den = jnp.sum(p, axis=-1, keepdims=True)
        if sink_ref is not None:
            den = den + jnp.exp(sink - mx)
        o = _mm(p.astype(BF16), v[:, grp * LANES:(grp + 1) * LANES]) / den
        outs.append(o)
    for j in range(N_HEADS // 2):
        o_ref[:, j * LANES:(j + 1) * LANES] = jnp.where(upper, outs[2 * j + 1], outs[2 * j]).astype(BF16)


def _window_attn_kernel(*refs, n_kblk, t_seq):
    q_ref, kc_ref, vc_ref = refs[:3]
    kw = refs[3:3 + n_kblk]
    vw = refs[3 + n_kblk:3 + 2 * n_kblk]
    sink_ref, o_ref = refs[3 + 2 * n_kblk:]
    qb = q_ref.shape[0]
    t_ctx = kc_ref.shape[0]
    k = jnp.concatenate([kc_ref[...]] + [r[...] for r in kw], axis=0)
    v = jnp.concatenate([vc_ref[...]] + [r[...] for r in vw], axis=0)
    nk = k.shape[0]
    start = pl.program_id(1) * qb
    row = lax.broadcasted_iota(jnp.int32, (qb, nk), 0)
    col = lax.broadcasted_iota(jnp.int32, (qb, nk), 1)
    rel = col - t_ctx - WINDOW
    pos = rel + start
    near = jnp.abs(rel - row) <= WINDOW
    mask = (col < t_ctx) | (near & (pos >= 0) & (pos < t_seq))
    _attend(q_ref, k, v, mask, sink_ref, o_ref)


def _window_attention(q, k, v, k_ctx, v_ctx, sink, bsz, t_seq, t_ctx):
    m = q.shape[0]
    qb = _pick_tile(t_seq, (256, 128))
    nq = t_seq // qb
    kblk_per_q = qb // WINDOW
    n_kblk = kblk_per_q + 2
    nkb_seq = t_seq // WINDOW
    kvw = k.shape[1]

    def win_spec(j):
        def idx(b, i):
            blk = jnp.clip(i * kblk_per_q + (j - 1), 0, nkb_seq - 1)
            return (b * nkb_seq + blk, 0)
        return pl.BlockSpec((WINDOW, kvw), idx)

    ctx_spec = pl.BlockSpec((t_ctx, kvw), lambda b, i: (b, 0))
    in_specs = ([pl.BlockSpec((qb, ATT_WIDTH), lambda b, i: (b * nq + i, 0)), ctx_spec, ctx_spec]
                + [win_spec(j) for j in range(n_kblk)] * 2
                + [pl.BlockSpec(memory_space=pltpu.SMEM)])
    return pl.pallas_call(
        functools.partial(_window_attn_kernel, n_kblk=n_kblk, t_seq=t_seq),
        grid=(bsz, nq),
        in_specs=in_specs,
        out_specs=pl.BlockSpec((qb, ATT_WIDTH), lambda b, i: (b * nq + i, 0)),
        out_shape=jax.ShapeDtypeStruct((m, ATT_WIDTH), BF16),
        compiler_params=_cparams(2),
        name="window_attn",
    )(q, k_ctx, v_ctx, *([k] * n_kblk), *([v] * n_kblk), sink)


def _dense_attn_kernel(*refs, n_kv_src, has_sink):
    q_ref = refs[0]
    k_refs = refs[1:1 + n_kv_src]
    v_refs = refs[1 + n_kv_src:1 + 2 * n_kv_src]
    rest = refs[1 + 2 * n_kv_src:]
    sink_ref = rest[0] if has_sink else None
    o_ref = rest[-1]
    if n_kv_src == 1:
        k, v = k_refs[0][...], v_refs[0][...]
    else:
        k = jnp.concatenate([r[...] for r in k_refs], axis=0)
        v = jnp.concatenate([r[...] for r in v_refs], axis=0)
    _attend(q_ref, k, v, None, sink_ref, o_ref)


def _dense_attention(q, ks, vs, sink, bsz, t_q):
    m = q.shape[0]
    qb = _pick_tile(t_q, (256, 128))
    nq = t_q // qb
    kv_specs = [pl.BlockSpec((a.shape[0] // bsz, a.shape[1]), lambda b, i: (b, 0)) for a in ks]
    in_specs = [pl.BlockSpec((qb, ATT_WIDTH), lambda b, i: (b * nq + i, 0))] + kv_specs * 2
    args = [q, *ks, *vs]
    if sink is not None:
        in_specs.append(pl.BlockSpec(memory_space=pltpu.SMEM))
        args.append(sink)
    return pl.pallas_call(
        functools.partial(_dense_attn_kernel, n_kv_src=len(ks), has_sink=sink is not None),
        grid=(bsz, nq),
        in_specs=in_specs,
        out_specs=pl.BlockSpec((qb, ATT_WIDTH), lambda b, i: (b * nq + i, 0)),
        out_shape=jax.ShapeDtypeStruct((m, ATT_WIDTH), BF16),
        compiler_params=_cparams(2),
        name="dense_attn",
    )(*args)


CONV_HALO = SUBLANES


def _conv_kernel(x_ref, prev_ref, next_ref, w_ref, b_ref, q_ref, kt_ref, xp):
    i = pl.program_id(1)
    n = pl.num_programs(1)
    tt = x_ref.shape[0]
    zero = jnp.zeros((CONV_HALO, x_ref.shape[1]), F32)
    xp[0:CONV_HALO] = jnp.where(i > 0, prev_ref[...], zero)
    xp[CONV_HALO:CONV_HALO + tt] = x_ref[...]
    xp[CONV_HALO + tt:] = jnp.where(i < n - 1, next_ref[...], zero)
    acc = b_ref[...] + w_ref[0:1] * xp[pl.ds(CONV_HALO - C_CONV // 2, tt)]
    for j in range(1, C_CONV):
        acc = acc + w_ref[j:j + 1] * xp[pl.ds(CONV_HALO - C_CONV // 2 + j, tt)]
    y = _silu(acc)
    q_ref[...] = (y[:, :C_WIDTH] * (C_HEAD_DIM ** -0.5)).astype(BF16)
    kt_ref[...] = y[:, C_WIDTH:].T.astype(BF16)


def _mlstm_conv(qk, conv_w, conv_b, bsz, t_seq):
    m, c = qk.shape
    tt = _pick_tile(t_seq, (256, 128))
    nt = t_seq // tt
    hb = tt // CONV_HALO
    nhb = t_seq // CONV_HALO
    return pl.pallas_call(
        _conv_kernel,
        grid=(bsz, nt),
        in_specs=[
            pl.BlockSpec((tt, c), lambda b, i: (b * nt + i, 0)),
            pl.BlockSpec((CONV_HALO, c), lambda b, i: (b * nhb + jnp.maximum(i * hb - 1, 0), 0)),
            pl.BlockSpec((CONV_HALO, c), lambda b, i: (b * nhb + jnp.minimum((i + 1) * hb, nhb - 1), 0)),
            _resident(conv_w.shape),
            _resident((1, c)),
        ],
        out_specs=[pl.BlockSpec((tt, C_WIDTH), lambda b, i: (b * nt + i, 0)),
                   pl.BlockSpec((C_WIDTH, tt), lambda b, i: (0, b * nt + i))],
        out_shape=[jax.ShapeDtypeStruct((m, C_WIDTH), BF16), jax.ShapeDtypeStruct((C_WIDTH, m), BF16)],
        scratch_shapes=[pltpu.VMEM((tt + 2 * CONV_HALO, c), F32)],
        compiler_params=_cparams(2),
        name="mlstm_conv",
    )(qk, qk, qk, conv_w, conv_b.reshape(1, c))


def _split3(x):
    hi = x.astype(BF16)
    r1 = x - hi.astype(F32)
    mid = r1.astype(BF16)
    lo = (r1 - mid.astype(F32)).astype(BF16)
    return hi, mid, lo


PAIR = 2 * CHUNK


def _mlstm_kernel(q_ref, kt_ref, v_ref, g_ref, gt_ref, gb_ref, gbt_ref, s0_ref, m0_ref,
                  h_ref, s_out, m_out, s_scr, m_scr, *, rev):
    gi = pl.program_id(1)
    n_g = pl.num_programs(1)
    blk = q_ref.shape[0]

    @pl.when(gi == 0)
    def _():
        s_scr[...] = s0_ref[0]
        m_scr[...] = m0_ref[0]

    gates = g_ref[...] + gb_ref[...]
    gates_t = gt_ref[...] + gbt_ref[...]
    lf = _log_sigmoid(gates)
    lf_t = _log_sigmoid(gates_t)
    r = lax.broadcasted_iota(jnp.int32, (blk, blk), 0)
    c = lax.broadcasted_iota(jnp.int32, (blk, blk), 1)
    same_chunk = (r // CHUNK) == (c // CHUNK)
    tri = (same_chunk & ((c >= r) if rev else (c <= r)))
    tri_b = jnp.where(tri, 1.0, 0.0).astype(BF16)
    bcol = sum(_mm(tri_b, piece) for piece in _split3(lf))
    brow = sum(_mm_nt(piece, tri_b) for piece in _split3(lf_t))

    pr = lax.broadcasted_iota(jnp.int32, (PAIR, PAIR), 0)
    pc = lax.broadcasted_iota(jnp.int32, (PAIR, PAIR), 1)
    pair_mask = ((pr // CHUNK) == (pc // CHUNK)) & ((pc >= pr) if rev else (pc <= pr))
    rowi = lax.broadcasted_iota(jnp.int32, (PAIR, 1), 0)
    in_first = (rowi >= CHUNK) if rev else (rowi < CHUNK)
    first = slice(CHUNK, PAIR) if rev else slice(0, CHUNK)
    second = slice(0, CHUNK) if rev else slice(CHUNK, PAIR)
    end1 = CHUNK if rev else CHUNK - 1
    end2 = 0 if rev else PAIR - 1
    in_first_full = jnp.broadcast_to(in_first, (PAIR, 2 * C_HEAD_DIM))
    ones = jnp.ones((PAIR, C_HEAD_DIM), F32)

    n_pairs = blk // PAIR
    for p in (range(n_pairs - 1, -1, -1) if rev else range(n_pairs)):
        rows = slice(p * PAIR, (p + 1) * PAIR)
        for hd in range(C_HEADS):
            ci = (2 * C_HEADS if rev else 0) + hd
            cf = ci + C_HEADS
            lanes = slice(hd * C_HEAD_DIM, (hd + 1) * C_HEAD_DIM)
            bc = bcol[rows, cf:cf + 1]
            ic = gates[rows, ci:ci + 1]
            br = brow[cf:cf + 1, rows]
            ir = gates_t[ci:ci + 1, rows]
            dlog = jnp.where(pair_mask, bc - br + ir, -jnp.inf)
            rowmax = jnp.max(dlog, axis=-1, keepdims=True)
            m_prev = m_scr[hd][0:1, 0:1]
            inter1 = bc[first] + m_prev
            mt1 = jnp.maximum(inter1, rowmax[first])
            m1 = mt1[end1 - first.start:end1 - first.start + 1]
            inter2 = bc[second] + m1
            mt2 = jnp.maximum(inter2, rowmax[second])
            m2 = mt2[end2 - second.start:end2 - second.start + 1]
            if rev:
                m_t = jnp.concatenate([mt2, mt1], axis=0)
                inter = jnp.concatenate([inter2, inter1], axis=0)
            else:
                m_t = jnp.concatenate([mt1, mt2], axis=0)
                inter = jnp.concatenate([inter1, inter2], axis=0)
            b_end1 = bc[end1:end1 + 1]
            b_end2 = bc[end2:end2 + 1]
            b_end = jnp.where(in_first, b_end1, b_end2)
            m_new = jnp.where(in_first, m1, m2)
            w_end = jnp.exp(b_end - bc + ic - m_new)
            decay1 = jnp.exp(b_end1 + m_prev - m1)
            decay2 = jnp.exp(b_end2 + m1 - m2)
            w_intra = jnp.exp(dlog - m_t)
            w_inter = jnp.exp(inter - m_t)

            qh = q_ref[rows, lanes]
            kth = kt_ref[lanes, rows]
            v_aug = jnp.concatenate([v_ref[rows, lanes].astype(F32), ones], axis=1)
            s = (_mm(qh, kth) * w_intra).astype(BF16)
            intra = _mm(s, v_aug.astype(BF16))
            xa = w_end * v_aug
            x1 = jnp.where(in_first_full, xa, 0.0).astype(BF16)
            x2 = jnp.where(in_first_full, 0.0, xa).astype(BF16)
            st0 = s_scr[hd]
            st1 = decay1 * st0 + _mm(kth, x1)
            st2 = decay2 * st1 + _mm(kth, x2)
            q_st1 = _mm(qh[first], st0.astype(BF16))
            q_st2 = _mm(qh[second], st1.astype(BF16))
            q_st = jnp.concatenate([q_st2, q_st1] if rev else [q_st1, q_st2], axis=0)
            tot = intra + w_inter * q_st
            num = tot[:, :C_HEAD_DIM]
            den = tot[:, C_HEAD_DIM:]
            h_ref[rows, lanes] = num / jnp.maximum(jnp.abs(den), jnp.exp(-m_t))
            s_scr[hd] = st2
            m_scr[hd] = jnp.broadcast_to(m2, (SUBLANES, LANES))

    @pl.when(gi == n_g - 1)
    def _():
        s_out[0] = s_scr[...]
        m_out[0] = m_scr[...]


def _mlstm_scan(q, kt, v, g, gt, gate_b, state, m_state, bsz, t_seq, rev):
    m = q.shape[0]
    blk = _pick_tile(t_seq, (256, 128))
    ng = t_seq // blk
    if rev:
        tok = lambda b, i: (b * ng + (ng - 1 - i), 0)
        tok_t = lambda b, i: (0, b * ng + (ng - 1 - i))
    else:
        tok = lambda b, i: (b * ng + i, 0)
        tok_t = lambda b, i: (0, b * ng + i)
    st_spec = pl.BlockSpec((1, C_HEADS, C_HEAD_DIM, 2 * C_HEAD_DIM), lambda b, i: (b, 0, 0, 0))
    m_spec = pl.BlockSpec((1, C_HEADS, SUBLANES, LANES), lambda b, i: (b, 0, 0, 0))
    return pl.pallas_call(
        functools.partial(_mlstm_kernel, rev=rev),
        grid=(bsz, ng),
        in_specs=[
            pl.BlockSpec((blk, C_WIDTH), tok),
            pl.BlockSpec((C_WIDTH, blk), tok_t),
            pl.BlockSpec((blk, C_WIDTH), tok),
            pl.BlockSpec((blk, N_GATES), tok),
            pl.BlockSpec((N_GATES, blk), tok_t),
            _resident((1, N_GATES)),
            _resident((N_GATES, 1)),
            st_spec,
            m_spec,
        ],
        out_specs=[pl.BlockSpec((blk, C_WIDTH), tok), st_spec, m_spec],
        out_shape=[jax.ShapeDtypeStruct((m, C_WIDTH), F32),
                   jax.ShapeDtypeStruct(state.shape, F32),
                   jax.ShapeDtypeStruct(m_state.shape, F32)],
        scratch_shapes=[pltpu.VMEM((C_HEADS, C_HEAD_DIM, 2 * C_HEAD_DIM), F32),
                        pltpu.VMEM((C_HEADS, SUBLANES, LANES), F32)],
        compiler_params=_cparams(2),
        name="mlstm_bwd" if rev else "mlstm_fwd",
    )(q, kt, v, g, gt, gate_b.reshape(1, N_GATES), gate_b.reshape(N_GATES, 1), state, m_state)


def _merge_kernel(x_ref, mod_ref, ng_ref, ya_ref, yb_ref, hf_ref, hb_ref, om_ref, hg_ref,
                  w_gl, w_br, w_out, o_ref):
    x = x_ref[...]
    mod = mod_ref[0]
    ng = ng_ref[...]
    h = _mod_norm(x, mod, ng, 1).astype(BF16)
    d = x.shape[1]
    hsum = hf_ref[...] + hb_ref[...]
    om = om_ref[...]
    ym = []
    for j in range(C_HEADS):
        lanes = slice(j * C_HEAD_DIM, (j + 1) * C_HEAD_DIM)
        ym.append((_sigmoid(om[:, lanes]) * _rms(hsum[:, lanes], hg_ref[:, lanes])).astype(BF16))
    ys = [ya_ref[...], yb_ref[...], jnp.concatenate(ym, axis=1)]
    acc = None
    for n in range(N_BRANCH):
        gate = _sigmoid(_mm(h, w_gl[:, n * d:(n + 1) * d]))
        z = gate * _mm(ys[n], w_br[n])
        acc = z if acc is None else acc + z
    y = _mm(acc.astype(BF16), w_out[...])
    o_ref[...] = _gated_residual(x, y, mod, ng, 1, 1.0)


def _merge(x, mod, ng, ya, yb, hf, hb, om, head_g, w_gl, w_br, w_out, rows_per_mod, mod_base):
    m, d = x.shape
    tm = _pick_tile(rows_per_mod)
    tiles_per_mod = rows_per_mod // tm
    row = lambda i: (i, 0)
    return pl.pallas_call(
        _merge_kernel,
        grid=(m // tm,),
        in_specs=[
            pl.BlockSpec((tm, d), row),
            pl.BlockSpec((1, N_MOD, d), lambda i: (mod_base + i // tiles_per_mod, 0, 0)),
            _resident(ng.shape),
            pl.BlockSpec((tm, ATT_WIDTH), row),
            pl.BlockSpec((tm, ATT_WIDTH), row),
            pl.BlockSpec((tm, C_WIDTH), row),
            pl.BlockSpec((tm, C_WIDTH), row),
            pl.BlockSpec((tm, C_WIDTH), row),
            _resident((1, C_WIDTH)),
            _resident(w_gl.shape),
            _resident(w_br.shape),
            _resident(w_out.shape),
        ],
        out_specs=pl.BlockSpec((tm, d), row),
        out_shape=jax.ShapeDtypeStruct((m, d), F32),
        compiler_params=_cparams(1),
        name="merge",
    )(x, mod, ng, ya, yb, hf, hb, om, head_g.reshape(1, C_WIDTH), w_gl, w_br, w_out)


def _rope_tables(t_seq):
    rows = t_seq // GRID_W
    row = jnp.repeat(jnp.arange(rows), GRID_W)
    col = jnp.tile(jnp.arange(GRID_W), rows)
    freqs = ROPE_THETA ** (-jnp.arange(ROPE_PAIRS, dtype=F32) / ROPE_PAIRS)
    ang = jnp.stack([row[:, None] * freqs, col[:, None] * freqs], axis=1)
    cos, sin = jnp.cos(ang), jnp.sin(ang)
    cos_head = jnp.concatenate([cos[:, 0], cos[:, 0], cos[:, 1], cos[:, 1]], axis=-1)
    sin_head = jnp.concatenate([-sin[:, 0], sin[:, 0], -sin[:, 1], sin[:, 1]], axis=-1)
    return jnp.tile(cos_head, (1, 2)), jnp.tile(sin_head, (1, 2))


def _split_in_weights(w_in):
    sizes = (ATT_WIDTH, N_KV * HEAD_DIM, N_KV * HEAD_DIM, ATT_WIDTH, N_KV * HEAD_DIM, N_KV * HEAD_DIM,
             C_WIDTH, C_WIDTH, C_WIDTH, C_WIDTH, N_GATES, N_BRANCH * w_in.shape[0])
    parts, start = [], 0
    for s in sizes:
        parts.append(w_in[:, start:start + s])
        start += s
    qa, ka, va, qb, kb, vb, qm, km, vm, om, gm, gl = parts

    def doubled(w):
        return jnp.concatenate([w[:, :HEAD_DIM], w[:, :HEAD_DIM], w[:, HEAD_DIM:], w[:, HEAD_DIM:]], axis=1)

    bf = lambda w: w.astype(BF16)
    weights = [bf(qa), bf(jnp.concatenate([doubled(ka), doubled(va)], axis=1)),
               bf(qb), bf(jnp.concatenate([doubled(kb), doubled(vb)], axis=1)),
               bf(jnp.concatenate([qm, km], axis=1)), bf(vm), bf(om), bf(gm), bf(gm.T)]
    return weights, bf(gl)


def kernel(x, c, ctx, c_ctx, w_ada, b_ada, norm_g, ffn_w_gate, ffn_w_up, ffn_w_down, w_in, attn_sink,
           qk_norm_g, conv_w, conv_b, mlstm_gate_b, mlstm_norm_g, w_branch, w_out):
    bsz, t_lat, d = x.shape
    t_ctx = ctx.shape[1]
    depth = w_ada.shape[0]
    xs = x.reshape(bsz * t_lat, d)
    cs = ctx.reshape(bsz * t_ctx, d)
    mods = _modulation(jnp.concatenate([c, c_ctx[None]], axis=0), w_ada, b_ada)
    rope_tabs = _rope_tables(t_lat)
    zero_state = jnp.zeros((bsz, C_HEADS, C_HEAD_DIM, 2 * C_HEAD_DIM), F32)
    zero_m = jnp.zeros((bsz, C_HEADS, SUBLANES, LANES), F32)

    for l in range(depth):
        last = l == depth - 1
        mod, ng = mods[l], norm_g[l]
        wg, wu, wd = (w[l].astype(BF16) for w in (ffn_w_gate, ffn_w_up, ffn_w_down))
        in_w, w_gl = _split_in_weights(w_in[l])
        qkn = jnp.tile(qk_norm_g[l], (1, 2))
        sink = attn_sink[l].reshape(1, N_HEADS)
        w_br, w_o = w_branch[l].astype(BF16), w_out[l].astype(BF16)
        lat = dict(rows_per_mod=t_lat, mod_base=0)
        con = dict(rows_per_mod=bsz * t_ctx, mod_base=bsz)

        xs = _half_ffn(xs, mod, ng, wg[0], wu[0], wd[0], 0, **lat)
        cs = _half_ffn(cs, mod, ng, wg[0], wu[0], wd[0], 0, **con)

        (qa, ka, va, qb, kb, vb, qkm, vm, om, gm, gmt) = _in_projection(
            xs, mod, ng, rope_tabs, qkn, in_w, t_seq=t_lat, **lat)
        (qa_c, ka_c, va_c, qb_c, kb_c, vb_c, qkm_c, vm_c, om_c, gm_c, gmt_c) = _in_projection(
            cs, mod, ng, None, qkn, in_w, t_seq=t_ctx, **con)

        ya = _window_attention(qa, ka, va, ka_c, va_c, sink, bsz, t_lat, t_ctx)
        yb = _dense_attention(qb, [kb_c, kb], [vb_c, vb], None, bsz, t_lat)

        qm_c, kmt_c = _mlstm_conv(qkm_c, conv_w[l], conv_b[l], bsz, t_ctx)
        qm, kmt = _mlstm_conv(qkm, conv_w[l], conv_b[l], bsz, t_lat)
        gb = mlstm_gate_b[l]
        hf_c, st_f, m_f = _mlstm_scan(qm_c, kmt_c, vm_c, gm_c, gmt_c, gb, zero_state, zero_m, bsz, t_ctx, False)
        hb_c, st_b, m_b = _mlstm_scan(qm_c, kmt_c, vm_c, gm_c, gmt_c, gb, zero_state, zero_m, bsz, t_ctx, True)
        hf, _, _ = _mlstm_scan(qm, kmt, vm, gm, gmt, gb, st_f, m_f, bsz, t_lat, False)
        hb, _, _ = _mlstm_scan(qm, kmt, vm, gm, gmt, gb, st_b, m_b, bsz, t_lat, True)

        xs = _merge(xs, mod, ng, ya, yb, hf, hb, om, mlstm_norm_g[l], w_gl, w_br, w_o, **lat)
        xs = _half_ffn(xs, mod, ng, wg[1], wu[1], wd[1], 2, **lat)
        if not last:
            ya_c = _dense_attention(qa_c, [ka_c], [va_c], sink, bsz, t_ctx)
            yb_c = _dense_attention(qb_c, [kb_c], [vb_c], None, bsz, t_ctx)
            cs = _merge(cs, mod, ng, ya_c, yb_c, hf_c, hb_c, om_c, mlstm_norm_g[l], w_gl, w_br, w_o, **con)
            cs = _half_ffn(cs, mod, ng, wg[1], wu[1], wd[1], 2, **con)
    return xs.reshape(bsz, t_lat, d)
```

```python
import functools

import jax
import jax.numpy as jnp
from jax import lax
from jax.experimental import pallas as pl
from jax.experimental.pallas import tpu as pltpu

F32 = jnp.float32
BF16 = jnp.bfloat16

EPS = 1e-6
N_MOD = 9
GRID_W = 64
ROPE_THETA = 10000.0
HEAD_DIM = 64
ROPE_PAIRS = HEAD_DIM // 4
N_HEADS = 8
N_KV = 2
ATT_WIDTH = N_HEADS * HEAD_DIM
WINDOW = 128
C_HEADS = 4
C_HEAD_DIM = 128
C_WIDTH = C_HEADS * C_HEAD_DIM
C_CONV = 5
CHUNK = 64
N_GATES = 4 * C_HEADS
N_BRANCH = 3
LOG2E = 1.4426950408889634

LANES = 128
SUBLANES = 8
VMEM_LIMIT_BYTES = 56 * 1024 * 1024


def _cparams(n_axes):
    return pltpu.CompilerParams(
        dimension_semantics=("arbitrary",) * n_axes,
        vmem_limit_bytes=VMEM_LIMIT_BYTES,
    )


def _pick_tile(n, candidates=(512, 256, 128)):
    for c in candidates:
        if n % c == 0:
            return c
    raise ValueError(f"no tile size for {n}")


def _resident(shape):
    zeros = (0,) * len(shape)
    return pl.BlockSpec(shape, lambda *_: zeros, pipeline_mode=pl.Buffered(1))


def _mm(a, b):
    return jnp.dot(a, b, preferred_element_type=F32)


def _mm_nt(a, b):
    return lax.dot_general(a, b, (((1,), (1,)), ((), ())), preferred_element_type=F32)


def _rms(x, g):
    return x * lax.rsqrt(jnp.mean(x * x, axis=-1, keepdims=True) + EPS) * g


def _sigmoid(x):
    return 1.0 / (1.0 + jnp.exp(-x))


def _silu(x):
    return x * _sigmoid(x)


def _log_sigmoid(x):
    return jnp.minimum(x, 0.0) - jnp.log(1.0 + jnp.exp(-jnp.abs(x)))


def _mod_norm(x, mod, ng, k):
    return _rms(x, ng[2 * k:2 * k + 1]) * (1.0 + mod[3 * k + 1:3 * k + 2]) + mod[3 * k:3 * k + 1]


def _gated_residual(x, y, mod, ng, k, weight):
    return x + (weight * mod[3 * k + 2:3 * k + 3]) * _rms(y, ng[2 * k + 1:2 * k + 2])


def _mod_kernel(c_ref, w_ref, b_ref, o_ref):
    a = _silu(c_ref[...]).astype(BF16)
    o_ref[0] = _mm(a, w_ref[0].astype(BF16)) + b_ref[0]


def _modulation(c_all, w_ada, b_ada):
    n_layers, d, nd = w_ada.shape
    r = c_all.shape[0]
    out = pl.pallas_call(
        _mod_kernel,
        grid=(n_layers, nd // d),
        in_specs=[
            pl.BlockSpec((r, d), lambda l, j: (0, 0)),
            pl.BlockSpec((1, d, d), lambda l, j: (l, 0, j)),
            pl.BlockSpec((1, 1, d), lambda l, j: (l, 0, j)),
        ],
        out_specs=pl.BlockSpec((1, r, d), lambda l, j: (l, 0, j)),
        out_shape=jax.ShapeDtypeStruct((n_layers, r, nd), F32),
        compiler_params=_cparams(2),
        name="adaln_mod",
    )(c_all, w_ada, b_ada.reshape(n_layers, 1, nd))
    return out.reshape(n_layers, r, N_MOD, d)


FFN_COL_CHUNK = 256


def _ffn_kernel(x_ref, mod_ref, ng_ref, wg_ref, wu_ref, wd_ref, o_ref, a_scr, *, k):
    x = x_ref[...]
    mod = mod_ref[0]
    ng = ng_ref[...]
    h = _mod_norm(x, mod, ng, k).astype(BF16)
    d_ff = wg_ref.shape[1]
    for j in range(d_ff // FFN_COL_CHUNK):
        cols = slice(j * FFN_COL_CHUNK, (j + 1) * FFN_COL_CHUNK)
        g = _mm(h, wg_ref[:, cols])
        u = _mm(h, wu_ref[:, cols])
        a_scr[:, cols] = (_silu(g) * u).astype(BF16)
    y = _mm(a_scr[...], wd_ref[...])
    o_ref[...] = _gated_residual(x, y, mod, ng, k, 0.5)


def _half_ffn(x, mod, ng, wg, wu, wd, k, rows_per_mod, mod_base):
    m, d = x.shape
    d_ff = wg.shape[1]
    tm = _pick_tile(rows_per_mod)
    tiles_per_mod = rows_per_mod // tm
    return pl.pallas_call(
        functools.partial(_ffn_kernel, k=k),
        grid=(m // tm,),
        in_specs=[
            pl.BlockSpec((tm, d), lambda i: (i, 0)),
            pl.BlockSpec((1, N_MOD, d), lambda i: (mod_base + i // tiles_per_mod, 0, 0)),
            _resident(ng.shape),
            _resident(wg.shape),
            _resident(wu.shape),
            _resident(wd.shape),
        ],
        out_specs=pl.BlockSpec((tm, d), lambda i: (i, 0)),
        out_shape=jax.ShapeDtypeStruct((m, d), F32),
        scratch_shapes=[pltpu.VMEM((tm, d_ff), BF16)],
        compiler_params=_cparams(1),
        name=f"half_ffn{k}",
    )(x, mod, ng, wg, wu, wd)


def _head_rms(x, gain):
    low = lax.broadcasted_iota(jnp.int32, x.shape, 1) < HEAD_DIM
    ss = x * x
    s_lo = jnp.sum(jnp.where(low, ss, 0.0), axis=-1, keepdims=True)
    s_hi = jnp.sum(jnp.where(low, 0.0, ss), axis=-1, keepdims=True)
    inv = jnp.where(low, lax.rsqrt(s_lo * (1.0 / HEAD_DIM) + EPS), lax.rsqrt(s_hi * (1.0 / HEAD_DIM) + EPS))
    return x * inv * gain


def _rope(x, cos, sin_signed):
    lane = lax.broadcasted_iota(jnp.int32, x.shape, 1)
    first_half = (lane & ROPE_PAIRS) == 0
    partner = jnp.where(first_half, pltpu.roll(x, LANES - ROPE_PAIRS, 1), pltpu.roll(x, ROPE_PAIRS, 1))
    return x * cos + partner * sin_signed


def _inproj_kernel(*refs, rope):
    if rope:
        x_ref, mod_ref, ng_ref, cos_ref, sin_ref = refs[:5]
        refs = refs[5:]
        cos, sin = cos_ref[...], sin_ref[...]
    else:
        x_ref, mod_ref, ng_ref = refs[:3]
        refs = refs[3:]
    (qkn_ref, w_qa, w_kva, w_qb, w_kvb, w_qkm, w_vm, w_om, w_g, w_gt,
     qa_ref, ka_ref, va_ref, qb_ref, kb_ref, vb_ref, qkm_ref, vm_ref, om_ref, g_ref, gt_ref) = refs
    h = _mod_norm(x_ref[...], mod_ref[0], ng_ref[...], 1).astype(BF16)
    qn, kn = qkn_ref[0:1], qkn_ref[1:2]
    kv_w = 2 * N_KV * HEAD_DIM

    def blocks(w_ref, first, count, norm_gain, scale):
        outs = []
        for j in range(first, first + count):
            blk = _mm(h, w_ref[:, j * LANES:(j + 1) * LANES])
            if norm_gain is not None:
                blk = _head_rms(blk, norm_gain)
            if rope:
                blk = _rope(blk, cos, sin)
            if scale != 1.0:
                blk = blk * scale
            outs.append(blk.astype(BF16))
        return outs

    q_scale = HEAD_DIM ** -0.5 * LOG2E
    for j, blk in enumerate(blocks(w_qa, 0, ATT_WIDTH // LANES, None, q_scale)):
        qa_ref[:, j * LANES:(j + 1) * LANES] = blk
    for j, blk in enumerate(blocks(w_kva, 0, kv_w // LANES, None, 1.0)):
        ka_ref[:, j * LANES:(j + 1) * LANES] = blk
    va_ref[...] = _mm(h, w_kva[:, kv_w:]).astype(BF16)
    for j, blk in enumerate(blocks(w_qb, 0, ATT_WIDTH // LANES, qn, q_scale)):
        qb_ref[:, j * LANES:(j + 1) * LANES] = blk
    for j, blk in enumerate(blocks(w_kvb, 0, kv_w // LANES, kn, 1.0)):
        kb_ref[:, j * LANES:(j + 1) * LANES] = blk
    vb_ref[...] = _mm(h, w_kvb[:, kv_w:]).astype(BF16)
    qkm_ref[...] = _mm(h, w_qkm[...])
    vm_ref[...] = _mm_nt(w_vm[...], h).astype(BF16)
    om_ref[...] = _mm(h, w_om[...])
    g_ref[...] = _mm(h, w_g[...])
    gt_ref[0] = _mm_nt(w_gt[...], h)


def _in_projection(x, mod, ng, rope_tabs, qkn, weights, rows_per_mod, mod_base, t_seq):
    m, d = x.shape
    tm = _pick_tile(min(rows_per_mod, t_seq))
    tiles_per_mod = rows_per_mod // tm
    tiles_per_seq = t_seq // tm
    rope = rope_tabs is not None
    row = lambda i: (i, 0)
    in_specs = [
        pl.BlockSpec((tm, d), row),
        pl.BlockSpec((1, N_MOD, d), lambda i: (mod_base + i // tiles_per_mod, 0, 0)),
        _resident(ng.shape),
    ]
    args = [x, mod, ng]
    if rope:
        in_specs +=[pl.BlockSpec((tm, LANES), lambda i: (i % tiles_per_seq, 0))] * 2
        args += list(rope_tabs)
    in_specs += [_resident(qkn.shape)] + [_resident(w.shape) for w in weights]
    args += [qkn] + list(weights)
    kv_w = 2 * N_KV * HEAD_DIM
    col = lambda i: (0, i)
    outs = [(ATT_WIDTH, BF16, False), (kv_w, BF16, False), (kv_w, BF16, False),
            (ATT_WIDTH, BF16, False), (kv_w, BF16, False), (kv_w, BF16, False),
            (2 * C_WIDTH, F32, False), (C_WIDTH, BF16, True), (C_WIDTH, F32, False),
            (N_GATES, F32, False)]
    out_specs = [pl.BlockSpec((w, tm), col) if fm else pl.BlockSpec((tm, w), row) for w, _, fm in outs]
    out_shape = [jax.ShapeDtypeStruct((w, m) if fm else (m, w), dt) for w, dt, fm in outs]
    out_specs.append(pl.BlockSpec((1, N_GATES, tm), lambda i: (i // tiles_per_seq, 0, i % tiles_per_seq)))
    out_shape.append(jax.ShapeDtypeStruct((m // t_seq, N_GATES, t_seq), F32))
    return pl.pallas_call(
        functools.partial(_inproj_kernel, rope=rope),
        grid=(m // tm,),
        in_specs=in_specs,
        out_specs=out_specs,
        out_shape=out_shape,
        compiler_params=_cparams(1),
        name="in_proj_rope" if rope else "in_proj",
    )(*args)


DENSE_SMALL_KEYS = 1024


def _attend(q_ref, rows, k, v, bias, sink_ref, o_ref):
    nr = rows.stop - rows.start
    nk = k.shape[0]
    gsz = N_HEADS // N_KV
    hpp = gsz if nk <= DENSE_SMALL_KEYS else 1
    upper = lax.broadcasted_iota(jnp.int32, (nr, LANES), 1) >= HEAD_DIM
    outs = []
    for first in range(0, N_HEADS, hpp):
        grp = first // gsz
        qs = []
        for hd in range(first, first + hpp):
            qblk = q_ref[rows, (hd // 2) * LANES:(hd // 2 + 1) * LANES]
            mine = upper if hd % 2 else jnp.logical_not(upper)
            qs.append(jnp.where(mine, qblk, jnp.zeros_like(qblk)))
        q_stack = qs[0] if hpp == 1 else jnp.concatenate(qs, axis=0)
        s = _mm_nt(q_stack, k[:, grp * LANES:(grp + 1) * LANES]).reshape(hpp, nr, nk)
        if bias is not None:
            s = s + bias[None]
        mx = jnp.max(s, axis=-1, keepdims=True)
        if sink_ref is not None:
            hidx = lax.broadcasted_iota(jnp.int32, (hpp, 1, 1), 0)
            sink = jnp.zeros((hpp, 1, 1), F32)
            for i in range(hpp):
                sink = jnp.where(hidx == i, sink_ref[0, first + i] * LOG2E, sink)
            mx = jnp.maximum(mx, sink)
        p = jnp.exp2(s - mx)
        den = jnp.sum(p, axis=-1, keepdims=True)
        if sink_ref is not None:
            den = den + jnp.exp2(sink - mx)
        o = _mm(p.astype(BF16).reshape(hpp * nr, nk), v[:, grp * LANES:(grp + 1) * LANES])
        o = o.reshape(hpp, nr, LANES) / den
        outs += [o[i] for i in range(hpp)]
    for j in range(N_HEADS // 2):
        o_ref[rows, j * LANES:(j + 1) * LANES] = jnp.where(upper, outs[2 * j + 1], outs[2 * j]).astype(BF16)


def _window_attn_kernel(*refs, n_kblk, t_seq):
    q_ref, kc_ref, vc_ref = refs[:3]
    kw = refs[3:3 + n_kblk]
    vw = refs[3 + n_kblk:3 + 2 * n_kblk]
    sink_ref, o_ref = refs[3 + 2 * n_kblk:]
    qb = q_ref.shape[0]
    t_ctx = kc_ref.shape[0]
    k = jnp.concatenate([kc_ref[...]] + [r[...] for r in kw], axis=0)
    v = jnp.concatenate([vc_ref[...]] + [r[...] for r in vw], axis=0)
    nk = k.shape[0]
    row = lax.broadcasted_iota(jnp.int32, (qb, nk), 0)
    col = lax.broadcasted_iota(jnp.int32, (qb, nk), 1)
    rel = col - t_ctx - WINDOW
    pos = rel + pl.program_id(1) * qb
    visible = (col < t_ctx) | ((jnp.abs(rel - row) <= WINDOW) & (pos >= 0) & (pos < t_seq))
    bias = jnp.where(visible, 0.0, -jnp.inf)
    _attend(q_ref, slice(0, qb), k, v, bias, sink_ref, o_ref)


def _window_attention(q, k, v, k_ctx, v_ctx, sink, bsz, t_seq, t_ctx):
    m = q.shape[0]
    qb = _pick_tile(t_seq, (256, 128))
    nq = t_seq // qb
    kblk_per_q = qb // WINDOW
    n_kblk = kblk_per_q + 2
    nkb_seq = t_seq // WINDOW
    kvw = k.shape[1]

    def win_spec(j):
        def idx(b, i):
            blk = jnp.clip(i * kblk_per_q + (j - 1), 0, nkb_seq - 1)
            return (b * nkb_seq + blk, 0)
        return pl.BlockSpec((WINDOW, kvw), idx)

    ctx_spec = pl.BlockSpec((t_ctx, kvw), lambda b, i: (b, 0))
    in_specs = ([pl.BlockSpec((qb, ATT_WIDTH), lambda b, i: (b * nq + i, 0)), ctx_spec, ctx_spec]
                + [win_spec(j) for j in range(n_kblk)] * 2
                + [pl.BlockSpec(memory_space=pltpu.SMEM)])
    return pl.pallas_call(
        functools.partial(_window_attn_kernel, n_kblk=n_kblk, t_seq=t_seq),
        grid=(bsz, nq),
        in_specs=in_specs,
        out_specs=pl.BlockSpec((qb, ATT_WIDTH), lambda b, i: (b * nq + i, 0)),
        out_shape=jax.ShapeDtypeStruct((m, ATT_WIDTH), BF16),
        compiler_params=_cparams(2),
        name="window_attn",
    )(q, k_ctx, v_ctx, *([k] * n_kblk), *([v] * n_kblk), sink)


def _dense_attn_kernel(*refs, n_kv_src, has_sink):
    q_ref = refs[0]
    k_refs = refs[1:1 + n_kv_src]
    v_refs = refs[1 + n_kv_src:1 + 2 * n_kv_src]
    rest = refs[1 + 2 * n_kv_src:]
    sink_ref = rest[0] if has_sink else None
    o_ref = rest[-1]
    if n_kv_src == 1:
        k, v = k_refs[0][...], v_refs[0][...]
    else:
        k = jnp.concatenate([r[...] for r in k_refs], axis=0)
        v = jnp.concatenate([r[...] for r in v_refs], axis=0)
    _attend(q_ref, slice(0, q_ref.shape[0]), k, v, None, sink_ref, o_ref)


def _dense_attention(q, ks, vs, sink, bsz, t_q):
    m = q.shape[0]
    qb = _pick_tile(t_q, (256, 128))
    nq = t_q // qb
    kv_specs = [pl.BlockSpec((a.shape[0] // bsz, a.shape[1]), lambda b, i: (b, 0)) for a in ks]
    in_specs = [pl.BlockSpec((qb, ATT_WIDTH), lambda b, i: (b * nq + i, 0))] + kv_specs * 2
    args = [q, *ks, *vs]
    if sink is not None:
        in_specs.append(pl.BlockSpec(memory_space=pltpu.SMEM))
        args.append(sink)
    return pl.pallas_call(
        functools.partial(_dense_attn_kernel, n_kv_src=len(ks), has_sink=sink is not None),
        grid=(bsz, nq),
        in_specs=in_specs,
        out_specs=pl.BlockSpec((qb, ATT_WIDTH), lambda b, i: (b * nq + i, 0)),
        out_shape=jax.ShapeDtypeStruct((m, ATT_WIDTH), BF16),
        compiler_params=_cparams(2),
        name="dense_attn",
    )(*args)


CONV_HALO = SUBLANES


def _conv_kernel(x_ref, prev_ref, next_ref, w_ref, b_ref, qt_ref, k_ref, xp):
    i = pl.program_id(1)
    n = pl.num_programs(1)
    tt = x_ref.shape[0]
    zero = jnp.zeros((CONV_HALO, x_ref.shape[1]), F32)
    xp[0:CONV_HALO] = jnp.where(i > 0, prev_ref[...], zero)
    xp[CONV_HALO:CONV_HALO + tt] = x_ref[...]
    xp[CONV_HALO + tt:] = jnp.where(i < n - 1, next_ref[...], zero)
    acc = b_ref[...] + w_ref[0:1] * xp[pl.ds(CONV_HALO - C_CONV // 2, tt)]
    for j in range(1, C_CONV):
        acc = acc + w_ref[j:j + 1] * xp[pl.ds(CONV_HALO - C_CONV // 2 + j, tt)]
    y = _silu(acc)
    qt_ref[...] = (y[:, :C_WIDTH] * (C_HEAD_DIM ** -0.5)).T.astype(BF16)
    k_ref[...] = y[:, C_WIDTH:].astype(BF16)


def _mlstm_conv(qk, conv_w, conv_b, bsz, t_seq):
    m, c = qk.shape
    tt = _pick_tile(t_seq, (256, 128))
    nt = t_seq // tt
    hb = tt // CONV_HALO
    nhb = t_seq // CONV_HALO
    return pl.pallas_call(
        _conv_kernel,
        grid=(bsz, nt),
        in_specs=[
            pl.BlockSpec((tt, c), lambda b, i: (b * nt + i, 0)),
            pl.BlockSpec((CONV_HALO, c), lambda b, i: (b * nhb + jnp.maximum(i * hb - 1, 0), 0)),
            pl.BlockSpec((CONV_HALO, c), lambda b, i: (b * nhb + jnp.minimum((i + 1) * hb, nhb - 1), 0)),
            _resident(conv_w.shape),
            _resident((1, c)),
        ],
        out_specs=[pl.BlockSpec((C_WIDTH, tt), lambda b, i: (0, b * nt + i)),
                   pl.BlockSpec((tt, C_WIDTH), lambda b, i: (b * nt + i, 0))],
        out_shape=[jax.ShapeDtypeStruct((C_WIDTH, m), BF16), jax.ShapeDtypeStruct((m, C_WIDTH), BF16)],
        scratch_shapes=[pltpu.VMEM((tt + 2 * CONV_HALO, c), F32)],
        compiler_params=_cparams(2),
        name="mlstm_conv",
    )(qk, qk, qk, conv_w, conv_b.reshape(1, c))


def _split3(x):
    hi = x.astype(BF16)
    r1 = x - hi.astype(F32)
    mid = r1.astype(BF16)
    lo = (r1 - mid.astype(F32)).astype(BF16)
    return hi, mid, lo


GATE_QUANTITIES = 5
GQ_A, GQ_INTER, GQ_NRM, GQ_END, GQ_DECAY = range(GATE_QUANTITIES)


def _mlstm_gate_kernel(gt_ref, gbt_ref, m0_ref, o_ref, m_out, m_scr, *, rev):
    gi = pl.program_id(0)
    n_g = pl.num_programs(0)
    bsz, _, blk = gt_ref.shape
    rows = bsz * N_GATES
    n_chunks = blk // CHUNK

    @pl.when(gi == 0)
    def _():
        m_scr[...] = m0_ref[...]

    gates_t = (gt_ref[...] + gbt_ref[...]).reshape(rows, blk)
    r = lax.broadcasted_iota(jnp.int32, (blk, blk), 0)
    c = lax.broadcasted_iota(jnp.int32, (blk, blk), 1)
    feeds_b = jnp.where((r >= c) if rev else (r <= c), 1.0, 0.0).astype(BF16)
    brow = sum(_mm(piece, feeds_b) for piece in _split3(_log_sigmoid(gates_t)))
    b_all = pltpu.roll(brow, rows - C_HEADS, 0)
    rp = gates_t - b_all
    lane = lax.broadcasted_iota(jnp.int32, (rows, blk), 1)
    pos = lane % CHUNK
    seg = rp
    shift = 1
    while shift < CHUNK:
        if rev:
            seg = jnp.where(pos < CHUNK - shift, jnp.maximum(seg, pltpu.roll(seg, blk - shift, 1)), seg)
        else:
            seg = jnp.where(pos >= shift, jnp.maximum(seg, pltpu.roll(seg, shift, 1)), seg)
        shift *= 2
    m_init = m_scr[:, 0:1]
    m_prev = m_init
    b_off = jnp.zeros((rows, 1), F32)
    m_lane = jnp.zeros((rows, blk), F32)
    off_lane = jnp.zeros((rows, blk), F32)
    for ch in (range(n_chunks - 1, -1, -1) if rev else range(n_chunks)):
        end = ch * CHUNK if rev else (ch + 1) * CHUNK - 1
        here = (lane // CHUNK) == ch
        m_lane = jnp.where(here, m_prev, m_lane)
        off_lane = jnp.where(here, b_off, off_lane)
        b_end = b_all[:, end:end + 1]
        m_prev = (b_end - b_off) + jnp.maximum(m_prev, b_off + seg[:, end:end + 1])
        b_off = b_end
    m_last, b_last = m_prev, b_off
    a = off_lane - jnp.maximum(m_lane, off_lane + seg)
    quantities = [None] * GATE_QUANTITIES
    quantities[GQ_A] = a
    quantities[GQ_INTER] = jnp.exp(a + m_init)
    quantities[GQ_NRM] = jnp.exp(a - b_all)
    quantities[GQ_END] = jnp.exp(rp + (b_last - m_last))
    quantities[GQ_DECAY] = jnp.broadcast_to(jnp.exp(b_last + m_init - m_last), (rows, blk))
    o_ref[...] = jnp.concatenate([x.reshape(bsz, N_GATES, blk) for x in quantities], axis=1)
    m_scr[...] = jnp.broadcast_to(m_last, (rows, LANES))

    @pl.when(gi == n_g - 1)
    def _():
        m_out[...] = m_scr[...]


def _mlstm_gates(gt, gate_b, m_state, t_seq, rev):
    bsz = gt.shape[0]
    blk = _pick_tile(t_seq, (256, 128))
    ng = t_seq // blk
    tok = (lambda i: (0, 0, ng - 1 - i)) if rev else (lambda i: (0, 0, i))
    return pl.pallas_call(
        functools.partial(_mlstm_gate_kernel, rev=rev),
        grid=(ng,),
        in_specs=[pl.BlockSpec((bsz, N_GATES, blk), tok), _resident((1, N_GATES, 1)), _resident(m_state.shape)],
        out_specs=[pl.BlockSpec((bsz, GATE_QUANTITIES * N_GATES, blk), tok), _resident(m_state.shape)],
        out_shape=[jax.ShapeDtypeStruct((bsz, GATE_QUANTITIES * N_GATES, t_seq), F32),
                   jax.ShapeDtypeStruct(m_state.shape, F32)],
        scratch_shapes=[pltpu.VMEM(m_state.shape, F32)],
        compiler_params=_cparams(1),
        name="mlstm_gates_bwd" if rev else "mlstm_gates_fwd",
    )(gt, gate_b.reshape(1, N_GATES, 1), m_state)


def _mlstm_kernel(k_ref, qt_ref, vt_ref, g_ref, gb_ref, gq_ref, s0_ref, h_ref, s_out, s_scr, *, rev):
    gi = pl.program_id(1)
    n_g = pl.num_programs(1)
    blk = k_ref.shape[0]

    @pl.when(gi == 0)
    def _():
        s_scr[...] = s0_ref[0]

    gates = g_ref[...] + gb_ref[...]
    r = lax.broadcasted_iota(jnp.int32, (blk, blk), 0)
    c = lax.broadcasted_iota(jnp.int32, (blk, blk), 1)
    feeds = (r >= c) if rev else (r <= c)
    feeds_tb = jnp.where((c >= r) if rev else (c <= r), 1.0, 0.0).astype(BF16)
    bcol = sum(_mm(feeds_tb, piece) for piece in _split3(_log_sigmoid(gates)))
    base = 2 * C_HEADS if rev else 0
    rp_col = gates[:, base:base + C_HEADS] - bcol[:, base + C_HEADS:base + 2 * C_HEADS]
    gq = gq_ref[0]
    ones = jnp.ones((C_HEAD_DIM, blk), BF16)

    for hd in range(C_HEADS):
        row = lambda q: gq[q * N_GATES + base + hd:q * N_GATES + base + hd + 1]
        lanes = slice(hd * C_HEAD_DIM, (hd + 1) * C_HEAD_DIM)
        kh = k_ref[:, lanes]
        qth = qt_ref[lanes, :]
        v_aug = jnp.concatenate([vt_ref[lanes, :], ones], axis=0)
        w_t = jnp.exp(jnp.where(feeds, rp_col[:, hd:hd + 1] + row(GQ_A), -jnp.inf))
        p = (_mm(kh, qth) * w_t).astype(BF16)
        st = s_scr[hd]
        tot = _mm(v_aug, p) + row(GQ_INTER) * _mm(st.astype(BF16), qth)
        h_t = tot[:C_HEAD_DIM] / jnp.maximum(jnp.abs(tot[C_HEAD_DIM:]), row(GQ_NRM))
        h_ref[:, lanes] = h_t.T
        x_t = (row(GQ_END) * v_aug.astype(F32)).astype(BF16)
        s_scr[hd] = row(GQ_DECAY)[:, :C_HEAD_DIM] * st + _mm(x_t, kh)

    @pl.when(gi == n_g - 1)
    def _():
        s_out[0] = s_scr[...]


def _mlstm_scan(k, qt, vt, g, gq, gate_b, state, bsz, t_seq, rev):
    m = k.shape[0]
    blk = _pick_tile(t_seq, (256, 128))
    ng = t_seq // blk
    step = (lambda i: ng - 1 - i) if rev else (lambda i: i)
    tok = lambda b, i: (b * ng + step(i), 0)
    tok_t = lambda b, i: (0, b * ng + step(i))
    st_spec = pl.BlockSpec((1, C_HEADS, 2 * C_HEAD_DIM, C_HEAD_DIM), lambda b, i: (b, 0, 0, 0))
    return pl.pallas_call(
        functools.partial(_mlstm_kernel, rev=rev),
        grid=(bsz, ng),
        in_specs=[
            pl.BlockSpec((blk, C_WIDTH), tok),
            pl.BlockSpec((C_WIDTH, blk), tok_t),
            pl.BlockSpec((C_WIDTH, blk), tok_t),
            pl.BlockSpec((blk, N_GATES), tok),
            _resident((1, N_GATES)),
            pl.BlockSpec((1, GATE_QUANTITIES * N_GATES, blk), lambda b, i: (b, 0, step(i))),
            st_spec,
        ],
        out_specs=[pl.BlockSpec((blk, C_WIDTH), tok), st_spec],
        out_shape=[jax.ShapeDtypeStruct((m, C_WIDTH), F32), jax.ShapeDtypeStruct(state.shape, F32)],
        scratch_shapes=[pltpu.VMEM((C_HEADS, 2 * C_HEAD_DIM, C_HEAD_DIM), F32)],
        compiler_params=_cparams(2),
        name="mlstm_bwd" if rev else "mlstm_fwd",
    )(k, qt, vt, g, gate_b.reshape(1, N_GATES), gq, state)


def _merge_kernel(x_ref, mod_ref, ng_ref, ya_ref, yb_ref, hf_ref, hb_ref, om_ref, hg_ref,
                  w_gl, w_br, w_out, o_ref):
    x = x_ref[...]
    mod = mod_ref[0]
    ng = ng_ref[...]
    h = _mod_norm(x, mod, ng, 1).astype(BF16)
    d = x.shape[1]
    hsum = hf_ref[...] + hb_ref[...]
    om = om_ref[...]
    ym = []
    for j in range(C_HEADS):
        lanes = slice(j * C_HEAD_DIM, (j + 1) * C_HEAD_DIM)
        ym.append((_sigmoid(om[:, lanes]) * _rms(hsum[:, lanes], hg_ref[:, lanes])).astype(BF16))
    ys = [ya_ref[...], yb_ref[...], jnp.concatenate(ym, axis=1)]
    acc = None
    for n in range(N_BRANCH):
        gate = _sigmoid(_mm(h, w_gl[:, n * d:(n + 1) * d]))
        z = gate * _mm(ys[n], w_br[n])
        acc = z if acc is None else acc + z
    y = _mm(acc.astype(BF16), w_out[...])
    o_ref[...] = _gated_residual(x, y, mod, ng, 1, 1.0)


def _merge(x, mod, ng, ya, yb, hf, hb, om, head_g, w_gl, w_br, w_out, rows_per_mod, mod_base):
    m, d = x.shape
    tm = _pick_tile(rows_per_mod)
    tiles_per_mod = rows_per_mod // tm
    row = lambda i: (i, 0)
    return pl.pallas_call(
        _merge_kernel,
        grid=(m // tm,),
        in_specs=[
            pl.BlockSpec((tm, d), row),
            pl.BlockSpec((1, N_MOD, d), lambda i: (mod_base + i // tiles_per_mod, 0, 0)),
            _resident(ng.shape),
            pl.BlockSpec((tm, ATT_WIDTH), row),
            pl.BlockSpec((tm, ATT_WIDTH), row),
            pl.BlockSpec((tm, C_WIDTH), row),
            pl.BlockSpec((tm, C_WIDTH), row),
            pl.BlockSpec((tm, C_WIDTH), row),
            _resident((1, C_WIDTH)),
            _resident(w_gl.shape),
            _resident(w_br.shape),
            _resident(w_out.shape),
        ],
        out_specs=pl.BlockSpec((tm, d), row),
        out_shape=jax.ShapeDtypeStruct((m, d), F32),
        compiler_params=_cparams(1),
        name="merge",
    )(x, mod, ng, ya, yb, hf, hb, om, head_g.reshape(1, C_WIDTH), w_gl, w_br, w_out)


def _rope_tables(t_seq):
    rows = t_seq // GRID_W
    row = jnp.repeat(jnp.arange(rows), GRID_W)
    col = jnp.tile(jnp.arange(GRID_W), rows)
    freqs = ROPE_THETA ** (-jnp.arange(ROPE_PAIRS, dtype=F32) / ROPE_PAIRS)
    ang = jnp.stack([row[:, None] * freqs, col[:, None] * freqs], axis=1)
    cos, sin = jnp.cos(ang), jnp.sin(ang)
    cos_head = jnp.concatenate([cos[:, 0], cos[:, 0], cos[:, 1], cos[:, 1]], axis=-1)
    sin_head = jnp.concatenate([-sin[:, 0], sin[:, 0], -sin[:, 1], sin[:, 1]], axis=-1)
    return jnp.tile(cos_head, (1, 2)), jnp.tile(sin_head, (1, 2))


def _split_in_weights(w_in):
    sizes = (ATT_WIDTH, N_KV * HEAD_DIM, N_KV * HEAD_DIM, ATT_WIDTH, N_KV * HEAD_DIM, N_KV * HEAD_DIM,
             C_WIDTH, C_WIDTH, C_WIDTH, C_WIDTH, N_GATES, N_BRANCH * w_in.shape[0])
    parts, start = [], 0
    for s in sizes:
        parts.append(w_in[:, start:start + s])
        start += s
    qa, ka, va, qb, kb, vb, qm, km, vm, om, gm, gl = parts

    def doubled(w):
        return jnp.concatenate([w[:, :HEAD_DIM], w[:, :HEAD_DIM], w[:, HEAD_DIM:], w[:, HEAD_DIM:]], axis=1)

    bf = lambda w: w.astype(BF16)
    weights = [bf(qa), bf(jnp.concatenate([doubled(ka), doubled(va)], axis=1)),
               bf(qb), bf(jnp.concatenate([doubled(kb), doubled(vb)], axis=1)),
               bf(jnp.concatenate([qm, km], axis=1)), bf(vm.T), bf(om), bf(gm), bf(gm.T)]
    return weights, bf(gl)


def kernel(x, c, ctx, c_ctx, w_ada, b_ada, norm_g, ffn_w_gate, ffn_w_up, ffn_w_down, w_in, attn_sink,
           qk_norm_g, conv_w, conv_b, mlstm_gate_b, mlstm_norm_g, w_branch, w_out):
    bsz, t_lat, d = x.shape
    t_ctx = ctx.shape[1]
    depth = w_ada.shape[0]
    xs = x.reshape(bsz * t_lat, d)
    cs = ctx.reshape(bsz * t_ctx, d)
    mods = _modulation(jnp.concatenate([c, c_ctx[None]], axis=0), w_ada, b_ada)
    rope_tabs = _rope_tables(t_lat)
    zero_state = jnp.zeros((bsz, C_HEADS, 2 * C_HEAD_DIM, C_HEAD_DIM), F32)
    zero_m = jnp.zeros((bsz * N_GATES, LANES), F32)

    for l in range(depth):
        last = l == depth - 1
        mod, ng = mods[l], norm_g[l]
        wg, wu, wd = (w[l].astype(BF16) for w in (ffn_w_gate, ffn_w_up, ffn_w_down))
        in_w, w_gl = _split_in_weights(w_in[l])
        qkn = jnp.tile(qk_norm_g[l], (1, 2))
        sink = attn_sink[l].reshape(1, N_HEADS)
        w_br, w_o = w_branch[l].astype(BF16), w_out[l].astype(BF16)
        lat = dict(rows_per_mod=t_lat, mod_base=0)
        con = dict(rows_per_mod=bsz * t_ctx, mod_base=bsz)

        xs = _half_ffn(xs, mod, ng, wg[0], wu[0], wd[0], 0, **lat)
        cs = _half_ffn(cs, mod, ng, wg[0], wu[0], wd[0], 0, **con)

        (qa, ka, va, qb, kb, vb, qkm, vm, om, gm, gmt) = _in_projection(
            xs, mod, ng, rope_tabs, qkn, in_w, t_seq=t_lat, **lat)
        (qa_c, ka_c, va_c, qb_c, kb_c, vb_c, qkm_c, vm_c, om_c, gm_c, gmt_c) = _in_projection(
            cs, mod, ng, None, qkn, in_w, t_seq=t_ctx, **con)

        ya = _window_attention(qa, ka, va, ka_c, va_c, sink, bsz, t_lat, t_ctx)
        yb = _dense_attention(qb, [kb_c, kb], [vb_c, vb], None, bsz, t_lat)

        qmt_c, km_c = _mlstm_conv(qkm_c, conv_w[l], conv_b[l], bsz, t_ctx)
        qmt, km = _mlstm_conv(qkm, conv_w[l], conv_b[l], bsz, t_lat)
        gb = mlstm_gate_b[l]
        gq_cf, m_f = _mlstm_gates(gmt_c, gb, zero_m, t_ctx, False)
        gq_cb, m_b = _mlstm_gates(gmt_c, gb, zero_m, t_ctx, True)
        gq_f, _ = _mlstm_gates(gmt, gb, m_f, t_lat, False)
        gq_b, _ = _mlstm_gates(gmt, gb, m_b, t_lat, True)
        hf_c, st_f = _mlstm_scan(km_c, qmt_c, vm_c, gm_c, gq_cf, gb, zero_state, bsz, t_ctx, False)
        hb_c, st_b = _mlstm_scan(km_c, qmt_c, vm_c, gm_c, gq_cb, gb, zero_state, bsz, t_ctx, True)
        hf, _ = _mlstm_scan(km, qmt, vm, gm, gq_f, gb, st_f, bsz, t_lat, False)
        hb, _ = _mlstm_scan(km, qmt, vm, gm, gq_b, gb, st_b, bsz, t_lat, True)

        xs = _merge(xs, mod, ng, ya, yb, hf, hb, om, mlstm_norm_g[l], w_gl, w_br, w_o, **lat)
        xs = _half_ffn(xs, mod, ng, wg[1], wu[1], wd[1], 2, **lat)
        if not last:
            ya_c = _dense_attention(qa_c, [ka_c], [va_c], sink, bsz, t_ctx)
            yb_c = _dense_attention(qb_c, [kb_c], [vb_c], None, bsz, t_ctx)
            cs = _merge(cs, mod, ng, ya_c, yb_c, hf_c, hb_c, om_c, mlstm_norm_g[l], w_gl, w_br, w_o, **con)
            cs = _half_ffn(cs, mod, ng, wg[1], wu[1], wd[1], 2, **con)
    return xs.reshape(bsz, t_lat, d)
```

```python
import functools

import jax
import jax.numpy as jnp
from jax import lax
from jax.experimental import pallas as pl
from jax.experimental.pallas import tpu as pltpu

F32 = jnp.float32
BF16 = jnp.bfloat16

EPS = 1e-6
N_MOD = 9
GRID_W = 64
ROPE_THETA = 10000.0
HEAD_DIM = 64
ROPE_PAIRS = HEAD_DIM // 4
N_HEADS = 8
N_KV = 2
ATT_WIDTH = N_HEADS * HEAD_DIM
WINDOW = 128
C_HEADS = 4
C_HEAD_DIM = 128
C_WIDTH = C_HEADS * C_HEAD_DIM
C_CONV = 5
CHUNK = 64
N_GATES = 4 * C_HEADS
N_BRANCH = 3
LOG2E = 1.4426950408889634

LANES = 128
SUBLANES = 8
VMEM_LIMIT_BYTES = 56 * 1024 * 1024


def _cparams(n_axes):
    return pltpu.CompilerParams(
        dimension_semantics=("arbitrary",) * n_axes,
        vmem_limit_bytes=VMEM_LIMIT_BYTES,
    )


def _pick_tile(n, candidates=(512, 256, 128)):
    for c in candidates:
        if n % c == 0:
            return c
    raise ValueError(f"no tile size for {n}")


def _resident(shape):
    zeros = (0,) * len(shape)
    return pl.BlockSpec(shape, lambda *_: zeros, pipeline_mode=pl.Buffered(1))


def _mm(a, b):
    return jnp.dot(a, b, preferred_element_type=F32)


def _mm_nt(a, b):
    return lax.dot_general(a, b, (((1,), (1,)), ((), ())), preferred_element_type=F32)


def _rms(x, g):
    return x * lax.rsqrt(jnp.mean(x * x, axis=-1, keepdims=True) + EPS) * g


def _sigmoid(x):
    return 1.0 / (1.0 + jnp.exp(-x))


def _silu(x):
    return x * _sigmoid(x)


def _log_sigmoid(x):
    return jnp.minimum(x, 0.0) - jnp.log(1.0 + jnp.exp(-jnp.abs(x)))


def _mod_norm(x, mod, ng, k):
    return _rms(x, ng[2 * k:2 * k + 1]) * (1.0 + mod[3 * k + 1:3 * k + 2]) + mod[3 * k:3 * k + 1]


def _gated_residual(x, y, mod, ng, k, weight):
    return x + (weight * mod[3 * k + 2:3 * k + 3]) * _rms(y, ng[2 * k + 1:2 * k + 2])


def _mod_kernel(c_ref, w_ref, b_ref, o_ref):
    a = _silu(c_ref[...]).astype(BF16)
    o_ref[0] = _mm(a, w_ref[0].astype(BF16)) + b_ref[0]


def _modulation(c_all, w_ada, b_ada):
    n_layers, d, nd = w_ada.shape
    r = c_all.shape[0]
    out = pl.pallas_call(
        _mod_kernel,
        grid=(n_layers, nd // d),
        in_specs=[
            pl.BlockSpec((r, d), lambda l, j: (0, 0)),
            pl.BlockSpec((1, d, d), lambda l, j: (l, 0, j)),
            pl.BlockSpec((1, 1, d), lambda l, j: (l, 0, j)),
        ],
        out_specs=pl.BlockSpec((1, r, d), lambda l, j: (l, 0, j)),
        out_shape=jax.ShapeDtypeStruct((n_layers, r, nd), F32),
        compiler_params=_cparams(2),
        name="adaln_mod",
    )(c_all, w_ada, b_ada.reshape(n_layers, 1, nd))
    return out.reshape(n_layers, r, N_MOD, d)


FFN_COL_CHUNK = 256


def _ffn_kernel(x_ref, mod_ref, ng_ref, wg_ref, wu_ref, wd_ref, o_ref, a_scr, *, k):
    x = x_ref[...]
    mod = mod_ref[0]
    ng = ng_ref[...]
    h = _mod_norm(x, mod, ng, k).astype(BF16)
    d_ff = wg_ref.shape[1]
    for j in range(d_ff // FFN_COL_CHUNK):
        cols = slice(j * FFN_COL_CHUNK, (j + 1) * FFN_COL_CHUNK)
        g = _mm(h, wg_ref[:, cols])
        u = _mm(h, wu_ref[:, cols])
        a_scr[:, cols] = (_silu(g) * u).astype(BF16)
    y = _mm(a_scr[...], wd_ref[...])
    o_ref[...] = _gated_residual(x, y, mod, ng, k, 0.5)


def _half_ffn(x, mod, ng, wg, wu, wd, k, rows_per_mod, mod_base):
    m, d = x.shape
    d_ff = wg.shape[1]
    tm = _pick_tile(rows_per_mod, (1024, 512, 256, 128))
    tiles_per_mod = rows_per_mod // tm
    return pl.pallas_call(
        functools.partial(_ffn_kernel, k=k),
        grid=(m // tm,),
        in_specs=[
            pl.BlockSpec((tm, d), lambda i: (i, 0)),
            pl.BlockSpec((1, N_MOD, d), lambda i: (mod_base + i // tiles_per_mod, 0, 0)),
            _resident(ng.shape),
            _resident(wg.shape),
            _resident(wu.shape),
            _resident(wd.shape),
        ],
        out_specs=pl.BlockSpec((tm, d), lambda i: (i, 0)),
        out_shape=jax.ShapeDtypeStruct((m, d), F32),
        scratch_shapes=[pltpu.VMEM((tm, d_ff), BF16)],
        compiler_params=_cparams(1),
        name=f"half_ffn{k}",
    )(x, mod, ng, wg, wu, wd)


def _head_rms(x, gain):
    low = lax.broadcasted_iota(jnp.int32, x.shape, 1) < HEAD_DIM
    ss = x * x
    s_lo = jnp.sum(jnp.where(low, ss, 0.0), axis=-1, keepdims=True)
    s_hi = jnp.sum(jnp.where(low, 0.0, ss), axis=-1, keepdims=True)
    inv = jnp.where(low, lax.rsqrt(s_lo * (1.0 / HEAD_DIM) + EPS), lax.rsqrt(s_hi * (1.0 / HEAD_DIM) + EPS))
    return x * inv * gain


def _rope(x, cos, sin_signed):
    lane = lax.broadcasted_iota(jnp.int32, x.shape, 1)
    first_half = (lane & ROPE_PAIRS) == 0
    partner = jnp.where(first_half, pltpu.roll(x, LANES - ROPE_PAIRS, 1), pltpu.roll(x, ROPE_PAIRS, 1))
    return x * cos + partner * sin_signed


CONV_HALO = SUBLANES


def _inproj_kernel(*refs, rope, tiles_per_seq):
    x_ref, prev_ref, next_ref, mod_ref, ng_ref = refs[:5]
    refs = refs[5:]
    if rope:
        cos, sin = refs[0][...], refs[1][...]
        refs = refs[2:]
    (qkn_ref, w_qa, w_kva, w_qb, w_kvb, w_qkm, w_vm, w_om, w_g, w_gt, cw_ref, cb_ref,
     qa_ref, ka_ref, va_ref, qb_ref, kb_ref, vb_ref, qmt_ref, km_ref, vm_ref, om_ref, g_ref, gt_ref, xp) = refs
    tm = x_ref.shape[0]
    mod, ng = mod_ref[0], ng_ref[...]
    h = _mod_norm(x_ref[...], mod, ng, 1).astype(BF16)
    qn, kn = qkn_ref[0:1], qkn_ref[1:2]
    kv_w = 2 * N_KV * HEAD_DIM

    def finish(full, out_ref, width, norm_gain, scale):
        for j in range(width // LANES):
            blk = full[:, j * LANES:(j + 1) * LANES]
            if norm_gain is not None:
                blk = _head_rms(blk, norm_gain)
            if rope:
                blk = _rope(blk, cos, sin)
            if scale != 1.0:
                blk = blk * scale
            out_ref[:, j * LANES:(j + 1) * LANES] = blk.astype(BF16)

    q_scale = HEAD_DIM ** -0.5 * LOG2E
    finish(_mm(h, w_qa[...]), qa_ref, ATT_WIDTH, None, q_scale)
    kva = _mm(h, w_kva[...])
    finish(kva, ka_ref, kv_w, None, 1.0)
    va_ref[...] = kva[:, kv_w:].astype(BF16)
    finish(_mm(h, w_qb[...]), qb_ref, ATT_WIDTH, qn, q_scale)
    kvb = _mm(h, w_kvb[...])
    finish(kvb, kb_ref, kv_w, kn, 1.0)
    vb_ref[...] = kvb[:, kv_w:].astype(BF16)
    vm_ref[...] = _mm_nt(w_vm[...], h).astype(BF16)
    om_ref[...] = _mm(h, w_om[...])
    g_ref[...] = _mm(h, w_g[...])
    gt_ref[0] = _mm_nt(w_gt[...], h)

    ti = pl.program_id(0) % tiles_per_seq
    halo = jnp.concatenate([prev_ref[...], next_ref[...]], axis=0)
    h_halo = _mod_norm(halo, mod, ng, 1).astype(BF16)
    qk = _mm(jnp.concatenate([h, h_halo], axis=0), w_qkm[...])
    zero = jnp.zeros((CONV_HALO, qk.shape[1]), F32)
    xp[0:CONV_HALO] = jnp.where(ti > 0, qk[tm:tm + CONV_HALO], zero)
    xp[CONV_HALO:CONV_HALO + tm] = qk[:tm]
    xp[CONV_HALO + tm:] = jnp.where(ti < tiles_per_seq - 1, qk[tm + CONV_HALO:], zero)
    first = CONV_HALO - C_CONV // 2
    acc = cb_ref[...] + cw_ref[0:1] * xp[first:first + tm]
    for j in range(1, C_CONV):
        acc = acc + cw_ref[j:j + 1] * xp[first + j:first + j + tm]
    y = _silu(acc)
    qmt_ref[...] = (y[:, :C_WIDTH] * (C_HEAD_DIM ** -0.5)).T.astype(BF16)
    km_ref[...] = y[:, C_WIDTH:].astype(BF16)


def _in_projection(x, mod, ng, rope_tabs, qkn, weights, conv_w, conv_b, rows_per_mod, mod_base, t_seq):
    m, d = x.shape
    tm = _pick_tile(min(rows_per_mod, t_seq))
    tiles_per_mod = rows_per_mod // tm
    tiles_per_seq = t_seq // tm
    rope = rope_tabs is not None
    row = lambda i: (i, 0)
    hb = tm // CONV_HALO
    n_hb = m // CONV_HALO
    in_specs = [
        pl.BlockSpec((tm, d), row),
        pl.BlockSpec((CONV_HALO, d), lambda i: (jnp.maximum(i * hb - 1, 0), 0)),
        pl.BlockSpec((CONV_HALO, d), lambda i: (jnp.minimum((i + 1) * hb, n_hb - 1), 0)),
        pl.BlockSpec((1, N_MOD, d), lambda i: (mod_base + i // tiles_per_mod, 0, 0)),
        _resident(ng.shape),
    ]
    args = [x, x, x, mod, ng]
    if rope:
        in_specs += [pl.BlockSpec((tm, LANES), lambda i: (i % tiles_per_seq, 0))] * 2
        args += list(rope_tabs)
    in_specs += [_resident(qkn.shape)] + [_resident(w.shape) for w in weights]
    in_specs += [_resident(conv_w.shape), _resident((1, conv_w.shape[1]))]
    args += [qkn] + list(weights) + [conv_w, conv_b.reshape(1, -1)]
    kv_w = 2 * N_KV * HEAD_DIM
    col = lambda i: (0, i)
    outs = [(ATT_WIDTH, BF16, False), (kv_w, BF16, False), (kv_w, BF16, False),
            (ATT_WIDTH, BF16, False), (kv_w, BF16, False), (kv_w, BF16, False),
            (C_WIDTH, BF16, True), (C_WIDTH, BF16, False), (C_WIDTH, BF16, True), (C_WIDTH, F32, False),
            (N_GATES, F32, False)]
    out_specs = [pl.BlockSpec((w, tm), col) if fm else pl.BlockSpec((tm, w), row) for w, _, fm in outs]
    out_shape = [jax.ShapeDtypeStruct((w, m) if fm else (m, w), dt) for w, dt, fm in outs]
    out_specs.append(pl.BlockSpec((1, N_GATES, tm), lambda i: (i // tiles_per_seq, 0, i % tiles_per_seq)))
    out_shape.append(jax.ShapeDtypeStruct((m // t_seq, N_GATES, t_seq), F32))
    return pl.pallas_call(
        functools.partial(_inproj_kernel, rope=rope, tiles_per_seq=tiles_per_seq),
        grid=(m // tm,),
        in_specs=in_specs,
        out_specs=out_specs,
        out_shape=out_shape,
        scratch_shapes=[pltpu.VMEM((tm + 2 * CONV_HALO, 2 * C_WIDTH), F32)],
        compiler_params=_cparams(1),
        name="in_proj_rope" if rope else "in_proj",
    )(*args)


DENSE_SMALL_KEYS = 1024


def _attend(q_ref, rows, k, v, bias, sink_ref, o_ref):
    nr = rows.stop - rows.start
    nk = k.shape[0]
    gsz = N_HEADS // N_KV
    hpp = gsz if nk <= DENSE_SMALL_KEYS else 1
    upper = lax.broadcasted_iota(jnp.int32, (nr, LANES), 1) >= HEAD_DIM
    outs = []
    for first in range(0, N_HEADS, hpp):
        grp = first // gsz
        qs = []
        for hd in range(first, first + hpp):
            qblk = q_ref[rows, (hd // 2) * LANES:(hd // 2 + 1) * LANES]
            mine = upper if hd % 2 else jnp.logical_not(upper)
            qs.append(jnp.where(mine, qblk, jnp.zeros_like(qblk)))
        q_stack = qs[0] if hpp == 1 else jnp.concatenate(qs, axis=0)
        s = _mm_nt(q_stack, k[:, grp * LANES:(grp + 1) * LANES]).reshape(hpp, nr, nk)
        if bias is not None:
            s = s + bias[None]
        mx = jnp.max(s, axis=-1, keepdims=True)
        if sink_ref is not None:
            hidx = lax.broadcasted_iota(jnp.int32, (hpp, 1, 1), 0)
            sink = jnp.zeros((hpp, 1, 1), F32)
            for i in range(hpp):
                sink = jnp.where(hidx == i, sink_ref[0, first + i] * LOG2E, sink)
            mx = jnp.maximum(mx, sink)
        p = jnp.exp2(s - mx)
        den = jnp.sum(p, axis=-1, keepdims=True)
        if sink_ref is not None:
            den = den + jnp.exp2(sink - mx)
        o = _mm(p.astype(BF16).reshape(hpp * nr, nk), v[:, grp * LANES:(grp + 1) * LANES])
        o = o.reshape(hpp, nr, LANES) / den
        outs += [o[i] for i in range(hpp)]
    for j in range(N_HEADS // 2):
        o_ref[rows, j * LANES:(j + 1) * LANES] = jnp.where(upper, outs[2 * j + 1], outs[2 * j]).astype(BF16)


def _window_attn_kernel(*refs, n_kblk, t_seq):
    q_ref, kc_ref, vc_ref = refs[:3]
    kw = refs[3:3 + n_kblk]
    vw = refs[3 + n_kblk:3 + 2 * n_kblk]
    sink_ref, o_ref = refs[3 + 2 * n_kblk:]
    qb = q_ref.shape[0]
    t_ctx = kc_ref.shape[0]
    k = jnp.concatenate([kc_ref[...]] + [r[...] for r in kw], axis=0)
    v = jnp.concatenate([vc_ref[...]] + [r[...] for r in vw], axis=0)
    nk = k.shape[0]
    row = lax.broadcasted_iota(jnp.int32, (qb, nk), 0)
    col = lax.broadcasted_iota(jnp.int32, (qb, nk), 1)
    rel = col - t_ctx - WINDOW
    pos = rel + pl.program_id(1) * qb
    visible = (col < t_ctx) | ((jnp.abs(rel - row) <= WINDOW) & (pos >= 0) & (pos < t_seq))
    bias = jnp.where(visible, 0.0, -jnp.inf)
    _attend(q_ref, slice(0, qb), k, v, bias, sink_ref, o_ref)


def _window_attention(q, k, v, k_ctx, v_ctx, sink, bsz, t_seq, t_ctx):
    m = q.shape[0]
    qb = _pick_tile(t_seq, (256, 128))
    nq = t_seq // qb
    kblk_per_q = qb // WINDOW
    n_kblk = kblk_per_q + 2
    nkb_seq = t_seq // WINDOW
    kvw = k.shape[1]

    def win_spec(j):
        def idx(b, i):
            blk = jnp.clip(i * kblk_per_q + (j - 1), 0, nkb_seq - 1)
            return (b * nkb_seq + blk, 0)
        return pl.BlockSpec((WINDOW, kvw), idx)

    ctx_spec = pl.BlockSpec((t_ctx, kvw), lambda b, i: (b, 0))
    in_specs = ([pl.BlockSpec((qb, ATT_WIDTH), lambda b, i: (b * nq + i, 0)), ctx_spec, ctx_spec]
                + [win_spec(j) for j in range(n_kblk)] * 2
                + [pl.BlockSpec(memory_space=pltpu.SMEM)])
    return pl.pallas_call(
        functools.partial(_window_attn_kernel, n_kblk=n_kblk, t_seq=t_seq),
        grid=(bsz, nq),
        in_specs=in_specs,
        out_specs=pl.BlockSpec((qb, ATT_WIDTH), lambda b, i: (b * nq + i, 0)),
        out_shape=jax.ShapeDtypeStruct((m, ATT_WIDTH), BF16),
        compiler_params=_cparams(2),
        name="window_attn",
    )(q, k_ctx, v_ctx, *([k] * n_kblk), *([v] * n_kblk), sink)


def _dense_attn_kernel(*refs, n_kv_src, has_sink):
    q_ref = refs[0]
    k_refs = refs[1:1 + n_kv_src]
    v_refs = refs[1 + n_kv_src:1 + 2 * n_kv_src]
    rest = refs[1 + 2 * n_kv_src:]
    sink_ref = rest[0] if has_sink else None
    o_ref = rest[-1]
    if n_kv_src == 1:
        k, v = k_refs[0][...], v_refs[0][...]
    else:
        k = jnp.concatenate([r[...] for r in k_refs], axis=0)
        v = jnp.concatenate([r[...] for r in v_refs], axis=0)
    _attend(q_ref, slice(0, q_ref.shape[0]), k, v, None, sink_ref, o_ref)


def _dense_attention(q, ks, vs, sink, bsz, t_q):
    m = q.shape[0]
    qb = _pick_tile(t_q, (256, 128))
    nq = t_q // qb
    kv_specs = [pl.BlockSpec((a.shape[0] // bsz, a.shape[1]), lambda b, i: (b, 0)) for a in ks]
    in_specs = [pl.BlockSpec((qb, ATT_WIDTH), lambda b, i: (b * nq + i, 0))] + kv_specs * 2
    args = [q, *ks, *vs]
    if sink is not None:
        in_specs.append(pl.BlockSpec(memory_space=pltpu.SMEM))
        args.append(sink)
    return pl.pallas_call(
        functools.partial(_dense_attn_kernel, n_kv_src=len(ks), has_sink=sink is not None),
        grid=(bsz, nq),
        in_specs=in_specs,
        out_specs=pl.BlockSpec((qb, ATT_WIDTH), lambda b, i: (b * nq + i, 0)),
        out_shape=jax.ShapeDtypeStruct((m, ATT_WIDTH), BF16),
        compiler_params=_cparams(2),
        name="dense_attn",
    )(*args)


def _split3(x):
    hi = x.astype(BF16)
    r1 = x - hi.astype(F32)
    mid = r1.astype(BF16)
    lo = (r1 - mid.astype(F32)).astype(BF16)
    return hi, mid, lo


GATE_QUANTITIES = 5
GQ_A, GQ_INTER, GQ_NRM, GQ_END, GQ_DECAY = range(GATE_QUANTITIES)


def _mlstm_gate_kernel(gt_ref, gbt_ref, m0_ref, o_ref, m_out, m_scr, *, rev):
    gi = pl.program_id(0)
    n_g = pl.num_programs(0)
    bsz, _, blk = gt_ref.shape
    rows = bsz * N_GATES
    n_chunks = blk // CHUNK

    @pl.when(gi == 0)
    def _():
        m_scr[...] = m0_ref[...]

    gates_t = (gt_ref[...] + gbt_ref[...]).reshape(rows, blk)
    r = lax.broadcasted_iota(jnp.int32, (blk, blk), 0)
    c = lax.broadcasted_iota(jnp.int32, (blk, blk), 1)
    feeds_b = jnp.where((r >= c) if rev else (r <= c), 1.0, 0.0).astype(BF16)
    brow = sum(_mm(piece, feeds_b) for piece in _split3(_log_sigmoid(gates_t)))
    b_all = pltpu.roll(brow, rows - C_HEADS, 0)
    rp = gates_t - b_all
    lane = lax.broadcasted_iota(jnp.int32, (rows, blk), 1)
    pos = lane % CHUNK
    seg = rp
    shift = 1
    while shift < CHUNK:
        if rev:
            seg = jnp.where(pos < CHUNK - shift, jnp.maximum(seg, pltpu.roll(seg, blk - shift, 1)), seg)
        else:
            seg = jnp.where(pos >= shift, jnp.maximum(seg, pltpu.roll(seg, shift, 1)), seg)
        shift *= 2
    m_init = m_scr[:, 0:1]
    m_prev = m_init
    b_off = jnp.zeros((rows, 1), F32)
    m_lane = jnp.zeros((rows, blk), F32)
    off_lane = jnp.zeros((rows, blk), F32)
    for ch in (range(n_chunks - 1, -1, -1) if rev else range(n_chunks)):
        end = ch * CHUNK if rev else (ch + 1) * CHUNK - 1
        here = (lane // CHUNK) == ch
        m_lane = jnp.where(here, m_prev, m_lane)
        off_lane = jnp.where(here, b_off, off_lane)
        b_end = b_all[:, end:end + 1]
        m_prev = (b_end - b_off) + jnp.maximum(m_prev, b_off + seg[:, end:end + 1])
        b_off = b_end
    m_last, b_last = m_prev, b_off
    a = off_lane - jnp.maximum(m_lane, off_lane + seg)
    quantities = [None] * GATE_QUANTITIES
    quantities[GQ_A] = a
    quantities[GQ_INTER] = jnp.exp(a + m_init)
    quantities[GQ_NRM] = jnp.exp(a - b_all)
    quantities[GQ_END] = jnp.exp(rp + (b_last - m_last))
    quantities[GQ_DECAY] = jnp.broadcast_to(jnp.exp(b_last + m_init - m_last), (rows, blk))
    o_ref[...] = jnp.concatenate([x.reshape(bsz, N_GATES, blk) for x in quantities], axis=1)
    m_scr[...] = jnp.broadcast_to(m_last, (rows, LANES))

    @pl.when(gi == n_g - 1)
    def _():
        m_out[...] = m_scr[...]


def _mlstm_gates(gt, gate_b, m_state, t_seq, rev):
    bsz = gt.shape[0]
    blk = _pick_tile(t_seq, (256, 128))
    ng = t_seq // blk
    tok = (lambda i: (0, 0, ng - 1 - i)) if rev else (lambda i: (0, 0, i))
    return pl.pallas_call(
        functools.partial(_mlstm_gate_kernel, rev=rev),
        grid=(ng,),
        in_specs=[pl.BlockSpec((bsz, N_GATES, blk), tok), _resident((1, N_GATES, 1)), _resident(m_state.shape)],
        out_specs=[pl.BlockSpec((bsz, GATE_QUANTITIES * N_GATES, blk), tok), _resident(m_state.shape)],
        out_shape=[jax.ShapeDtypeStruct((bsz, GATE_QUANTITIES * N_GATES, t_seq), F32),
                   jax.ShapeDtypeStruct(m_state.shape, F32)],
        scratch_shapes=[pltpu.VMEM(m_state.shape, F32)],
        compiler_params=_cparams(1),
        name="mlstm_gates_bwd" if rev else "mlstm_gates_fwd",
    )(gt, gate_b.reshape(1, N_GATES, 1), m_state)


N_DIR = 2


def _mlstm_kernel(*refs):
    per_dir = [refs[0:6], refs[6:12]]
    gb_ref = refs[12]
    h_refs, s_outs, s_scr = refs[13:15], refs[15:17], refs[17]
    gi = pl.program_id(1)
    n_g = pl.num_programs(1)
    blk = refs[0].shape[0]

    @pl.when(gi == 0)
    def _():
        for d in range(N_DIR):
            s_scr[d] = per_dir[d][5][0]

    r = lax.broadcasted_iota(jnp.int32, (blk, blk), 0)
    c = lax.broadcasted_iota(jnp.int32, (blk, blk), 1)
    ones = jnp.ones((C_HEAD_DIM, blk), BF16)
    for d, rev in enumerate((False, True)):
        k_ref, qt_ref, vt_ref, g_ref, gq_ref, _ = per_dir[d]
        gates = g_ref[...] + gb_ref[...]
        feeds = (r >= c) if rev else (r <= c)
        feeds_tb = jnp.where((c >= r) if rev else (c <= r), 1.0, 0.0).astype(BF16)
        bcol = sum(_mm(feeds_tb, piece) for piece in _split3(_log_sigmoid(gates)))
        base = 2 * C_HEADS if rev else 0
        rp_col = gates[:, base:base + C_HEADS] - bcol[:, base + C_HEADS:base + 2 * C_HEADS]
        gq = gq_ref[0]
        for hd in range(C_HEADS):
            row = lambda q: gq[q * N_GATES + base + hd:q * N_GATES + base + hd + 1]
            lanes = slice(hd * C_HEAD_DIM, (hd + 1) * C_HEAD_DIM)
            kh = k_ref[:, lanes]
            qth = qt_ref[lanes, :]
            v_aug = jnp.concatenate([vt_ref[lanes, :], ones], axis=0)
            w_t = jnp.exp(jnp.where(feeds, rp_col[:, hd:hd + 1] + row(GQ_A), -jnp.inf))
            p = (_mm(kh, qth) * w_t).astype(BF16)
            st = s_scr[d, hd]
            tot = _mm(v_aug, p) + row(GQ_INTER) * _mm(st.astype(BF16), qth)
            h_t = tot[:C_HEAD_DIM] / jnp.maximum(jnp.abs(tot[C_HEAD_DIM:]), row(GQ_NRM))
            h_refs[d][:, lanes] = h_t.T
            x_t = (row(GQ_END) * v_aug.astype(F32)).astype(BF16)
            s_scr[d, hd] = row(GQ_DECAY)[:, :C_HEAD_DIM] * st + _mm(x_t, kh)

    @pl.when(gi == n_g - 1)
    def _():
        for d in range(N_DIR):
            s_outs[d][0] = s_scr[d]


def _mlstm_scan(k, qt, vt, g, gqs, gate_b, states, bsz, t_seq):
    m = k.shape[0]
    blk = _pick_tile(t_seq, (256, 128))
    ng = t_seq // blk
    st_spec = pl.BlockSpec((1, C_HEADS, 2 * C_HEAD_DIM, C_HEAD_DIM), lambda b, i: (b, 0, 0, 0))
    in_specs, args, h_specs = [], [], []
    for d, step in enumerate((lambda i: i, lambda i: ng - 1 - i)):
        tok = lambda b, i, step=step: (b * ng + step(i), 0)
        tok_t = lambda b, i, step=step: (0, b * ng + step(i))
        in_specs += [
            pl.BlockSpec((blk, C_WIDTH), tok),
            pl.BlockSpec((C_WIDTH, blk), tok_t),
            pl.BlockSpec((C_WIDTH, blk), tok_t),
            pl.BlockSpec((blk, N_GATES), tok),
            pl.BlockSpec((1, GATE_QUANTITIES * N_GATES, blk), lambda b, i, step=step: (b, 0, step(i))),
            st_spec,
        ]
        args += [k, qt, vt, g, gqs[d], states[d]]
        h_specs.append(pl.BlockSpec((blk, C_WIDTH), tok))
    return pl.pallas_call(
        _mlstm_kernel,
        grid=(bsz, ng),
        in_specs=in_specs + [_resident((1, N_GATES))],
        out_specs=h_specs + [st_spec] * N_DIR,
        out_shape=[jax.ShapeDtypeStruct((m, C_WIDTH), F32)] * N_DIR
                  + [jax.ShapeDtypeStruct(states[0].shape, F32)] * N_DIR,
        scratch_shapes=[pltpu.VMEM((N_DIR, C_HEADS, 2 * C_HEAD_DIM, C_HEAD_DIM), F32)],
        compiler_params=_cparams(2),
        name="mlstm_scan",
    )(*args, gate_b.reshape(1, N_GATES))


def _merge_kernel(x_ref, mod_ref, ng_ref, ya_ref, yb_ref, hf_ref, hb_ref, om_ref, hg_ref,
                  w_gl, w_br, w_out, o_ref):
    x = x_ref[...]
    mod = mod_ref[0]
    ng = ng_ref[...]
    h = _mod_norm(x, mod, ng, 1).astype(BF16)
    d = x.shape[1]
    hsum = hf_ref[...] + hb_ref[...]
    om = om_ref[...]
    ym = []
    for j in range(C_HEADS):
        lanes = slice(j * C_HEAD_DIM, (j + 1) * C_HEAD_DIM)
        ym.append((_sigmoid(om[:, lanes]) * _rms(hsum[:, lanes], hg_ref[:, lanes])).astype(BF16))
    ys = [ya_ref[...], yb_ref[...], jnp.concatenate(ym, axis=1)]
    acc = None
    for n in range(N_BRANCH):
        gate = _sigmoid(_mm(h, w_gl[:, n * d:(n + 1) * d]))
        z = gate * _mm(ys[n], w_br[n])
        acc = z if acc is None else acc + z
    y = _mm(acc.astype(BF16), w_out[...])
    o_ref[...] = _gated_residual(x, y, mod, ng, 1, 1.0)


def _merge(x, mod, ng, ya, yb, hf, hb, om, head_g, w_gl, w_br, w_out, rows_per_mod, mod_base):
    m, d = x.shape
    tm = _pick_tile(rows_per_mod)
    tiles_per_mod = rows_per_mod // tm
    row = lambda i: (i, 0)
    return pl.pallas_call(
        _merge_kernel,
        grid=(m // tm,),
        in_specs=[
            pl.BlockSpec((tm, d), row),
            pl.BlockSpec((1, N_MOD, d), lambda i: (mod_base + i // tiles_per_mod, 0, 0)),
            _resident(ng.shape),
            pl.BlockSpec((tm, ATT_WIDTH), row),
            pl.BlockSpec((tm, ATT_WIDTH), row),
            pl.BlockSpec((tm, C_WIDTH), row),
            pl.BlockSpec((tm, C_WIDTH), row),
            pl.BlockSpec((tm, C_WIDTH), row),
            _resident((1, C_WIDTH)),
            _resident(w_gl.shape),
            _resident(w_br.shape),
            _resident(w_out.shape),
        ],
        out_specs=pl.BlockSpec((tm, d), row),
        out_shape=jax.ShapeDtypeStruct((m, d), F32),
        compiler_params=_cparams(1),
        name="merge",
    )(x, mod, ng, ya, yb, hf, hb, om, head_g.reshape(1, C_WIDTH), w_gl, w_br, w_out)


def _rope_tables(t_seq):
    rows = t_seq // GRID_W
    row = jnp.repeat(jnp.arange(rows), GRID_W)
    col = jnp.tile(jnp.arange(GRID_W), rows)
    freqs = ROPE_THETA ** (-jnp.arange(ROPE_PAIRS, dtype=F32) / ROPE_PAIRS)
    ang = jnp.stack([row[:, None] * freqs, col[:, None] * freqs], axis=1)
    cos, sin = jnp.cos(ang), jnp.sin(ang)
    cos_head = jnp.concatenate([cos[:, 0], cos[:, 0], cos[:, 1], cos[:, 1]], axis=-1)
    sin_head = jnp.concatenate([-sin[:, 0], sin[:, 0], -sin[:, 1], sin[:, 1]], axis=-1)
    return jnp.tile(cos_head, (1, 2)), jnp.tile(sin_head, (1, 2))


def _split_in_weights(w_in):
    sizes = (ATT_WIDTH, N_KV * HEAD_DIM, N_KV * HEAD_DIM, ATT_WIDTH, N_KV * HEAD_DIM, N_KV * HEAD_DIM,
             C_WIDTH, C_WIDTH, C_WIDTH, C_WIDTH, N_GATES, N_BRANCH * w_in.shape[0])
    parts, start = [], 0
    for s in sizes:
        parts.append(w_in[:, start:start + s])
        start += s
    qa, ka, va, qb, kb, vb, qm, km, vm, om, gm, gl = parts

    def doubled(w):
        return jnp.concatenate([w[:, :HEAD_DIM], w[:, :HEAD_DIM], w[:, HEAD_DIM:], w[:, HEAD_DIM:]], axis=1)

    bf = lambda w: w.astype(BF16)
    weights = [bf(qa), bf(jnp.concatenate([doubled(ka), doubled(va)], axis=1)),
               bf(qb), bf(jnp.concatenate([doubled(kb), doubled(vb)], axis=1)),
               bf(jnp.concatenate([qm, km], axis=1)), bf(vm.T), bf(om), bf(gm), bf(gm.T)]
    return weights, bf(gl)


def kernel(x, c, ctx, c_ctx, w_ada, b_ada, norm_g, ffn_w_gate, ffn_w_up, ffn_w_down, w_in, attn_sink,
           qk_norm_g, conv_w, conv_b, mlstm_gate_b, mlstm_norm_g, w_branch, w_out):
    bsz, t_lat, d = x.shape
    t_ctx = ctx.shape[1]
    depth = w_ada.shape[0]
    xs = x.reshape(bsz * t_lat, d)
    cs = ctx.reshape(bsz * t_ctx, d)
    mods = _modulation(jnp.concatenate([c, c_ctx[None]], axis=0), w_ada, b_ada)
    rope_tabs = _rope_tables(t_lat)
    zero_state = jnp.zeros((bsz, C_HEADS, 2 * C_HEAD_DIM, C_HEAD_DIM), F32)
    zero_m = jnp.zeros((bsz * N_GATES, LANES), F32)

    for l in range(depth):
        last = l == depth - 1
        mod, ng = mods[l], norm_g[l]
        wg, wu, wd = (w[l].astype(BF16) for w in (ffn_w_gate, ffn_w_up, ffn_w_down))
        in_w, w_gl = _split_in_weights(w_in[l])
        qkn = jnp.tile(qk_norm_g[l], (1, 2))
        sink = attn_sink[l].reshape(1, N_HEADS)
        w_br, w_o = w_branch[l].astype(BF16), w_out[l].astype(BF16)
        lat = dict(rows_per_mod=t_lat, mod_base=0)
        con = dict(rows_per_mod=bsz * t_ctx, mod_base=bsz)

        xs = _half_ffn(xs, mod, ng, wg[0], wu[0], wd[0], 0, **lat)
        cs = _half_ffn(cs, mod, ng, wg[0], wu[0], wd[0], 0, **con)

        (qa, ka, va, qb, kb, vb, qmt, km, vm, om, gm, gmt) = _in_projection(
            xs, mod, ng, rope_tabs, qkn, in_w, conv_w[l], conv_b[l], t_seq=t_lat, **lat)
        (qa_c, ka_c, va_c, qb_c, kb_c, vb_c, qmt_c, km_c, vm_c, om_c, gm_c, gmt_c) = _in_projection(
            cs, mod, ng, None, qkn, in_w, conv_w[l], conv_b[l], t_seq=t_ctx, **con)

        ya = _window_attention(qa, ka, va, ka_c, va_c, sink, bsz, t_lat, t_ctx)
        yb = _dense_attention(qb, [kb_c, kb], [vb_c, vb], None, bsz, t_lat)

        gb = mlstm_gate_b[l]
        gq_cf, m_f = _mlstm_gates(gmt_c, gb, zero_m, t_ctx, False)
        gq_cb, m_b = _mlstm_gates(gmt_c, gb, zero_m, t_ctx, True)
        gq_f, _ = _mlstm_gates(gmt, gb, m_f, t_lat, False)
        gq_b, _ = _mlstm_gates(gmt, gb, m_b, t_lat, True)
        hf_c, hb_c, st_f, st_b = _mlstm_scan(km_c, qmt_c, vm_c, gm_c, (gq_cf, gq_cb), gb,
                                             (zero_state, zero_state), bsz, t_ctx)
        hf, hb, _, _ = _mlstm_scan(km, qmt, vm, gm, (gq_f, gq_b), gb, (st_f, st_b), bsz, t_lat)

        xs = _merge(xs, mod, ng, ya, yb, hf, hb, om, mlstm_norm_g[l], w_gl, w_br, w_o, **lat)
        xs = _half_ffn(xs, mod, ng, wg[1], wu[1], wd[1], 2, **lat)
        if not last:
            ya_c = _dense_attention(qa_c, [ka_c], [va_c], sink, bsz, t_ctx)
            yb_c = _dense_attention(qb_c, [kb_c], [vb_c], None, bsz, t_ctx)
            cs = _merge(cs, mod, ng, ya_c, yb_c, hf_c, hb_c, om_c, mlstm_norm_g[l], w_gl, w_br, w_o, **con)
            cs = _half_ffn(cs, mod, ng, wg[1], wu[1], wd[1], 2, **con)
    return xs.reshape(bsz, t_lat, d)
```

```python
import functools

import jax
import jax.numpy as jnp
from jax import lax
from jax.experimental import pallas as pl
from jax.experimental.pallas import tpu as pltpu

F32 = jnp.float32
BF16 = jnp.bfloat16

EPS = 1e-6
N_MOD = 9
GRID_W = 64
ROPE_THETA = 10000.0
HEAD_DIM = 64
ROPE_PAIRS = HEAD_DIM // 4
N_HEADS = 8
N_KV = 2
ATT_WIDTH = N_HEADS * HEAD_DIM
WINDOW = 128
C_HEADS = 4
C_HEAD_DIM = 128
C_WIDTH = C_HEADS * C_HEAD_DIM
C_CONV = 5
CHUNK = 64
N_GATES = 4 * C_HEADS
N_BRANCH = 3
LOG2E = 1.4426950408889634

LANES = 128
SUBLANES = 8
VMEM_LIMIT_BYTES = 56 * 1024 * 1024


def _cparams(n_axes):
    return pltpu.CompilerParams(
        dimension_semantics=("arbitrary",) * n_axes,
        vmem_limit_bytes=VMEM_LIMIT_BYTES,
    )


def _pick_tile(n, candidates=(512, 256, 128)):
    for c in candidates:
        if n % c == 0:
            return c
    raise ValueError(f"no tile size for {n}")


def _resident(shape):
    zeros = (0,) * len(shape)
    return pl.BlockSpec(shape, lambda *_: zeros, pipeline_mode=pl.Buffered(1))


def _mm(a, b):
    return jnp.dot(a, b, preferred_element_type=F32)


def _mm_nt(a, b):
    return lax.dot_general(a, b, (((1,), (1,)), ((), ())), preferred_element_type=F32)


def _rms(x, g):
    return x * lax.rsqrt(jnp.mean(x * x, axis=-1, keepdims=True) + EPS) * g


def _sigmoid(x):
    return 1.0 / (1.0 + jnp.exp(-x))


def _silu(x):
    return x * _sigmoid(x)


def _log_sigmoid(x):
    return jnp.minimum(x, 0.0) - jnp.log(1.0 + jnp.exp(-jnp.abs(x)))


def _mod_norm(x, mod, ng, k):
    return _rms(x, ng[2 * k:2 * k + 1]) * (1.0 + mod[3 * k + 1:3 * k + 2]) + mod[3 * k:3 * k + 1]


def _gated_residual(x, y, mod, ng, k, weight):
    return x + (weight * mod[3 * k + 2:3 * k + 3]) * _rms(y, ng[2 * k + 1:2 * k + 2])


def _mod_kernel(c_ref, w_ref, b_ref, o_ref):
    a = _silu(c_ref[...]).astype(BF16)
    o_ref[0] = _mm(a, w_ref[0].astype(BF16)) + b_ref[0]


def _modulation(c_all, w_ada, b_ada):
    n_layers, d, nd = w_ada.shape
    r = c_all.shape[0]
    out = pl.pallas_call(
        _mod_kernel,
        grid=(n_layers, nd // d),
        in_specs=[
            pl.BlockSpec((r, d), lambda l, j: (0, 0)),
            pl.BlockSpec((1, d, d), lambda l, j: (l, 0, j)),
            pl.BlockSpec((1, 1, d), lambda l, j: (l, 0, j)),
        ],
        out_specs=pl.BlockSpec((1, r, d), lambda l, j: (l, 0, j)),
        out_shape=jax.ShapeDtypeStruct((n_layers, r, nd), F32),
        compiler_params=_cparams(2),
        name="adaln_mod",
    )(c_all, w_ada, b_ada.reshape(n_layers, 1, nd))
    return out.reshape(n_layers, r, N_MOD, d)


FFN_COL_CHUNK = 256


def _ffn_kernel(x_ref, mod_ref, ng_ref, wg_ref, wu_ref, wd_ref, o_ref, a_scr, *, k):
    x = x_ref[...]
    mod = mod_ref[0]
    ng = ng_ref[...]
    h = _mod_norm(x, mod, ng, k).astype(BF16)
    n_chunks = wg_ref.shape[1] // FFN_COL_CHUNK
    cols = [slice(j * FFN_COL_CHUNK, (j + 1) * FFN_COL_CHUNK) for j in range(n_chunks)]
    gate_up = lambda j: (_mm(h, wg_ref[:, cols[j]]), _mm(h, wu_ref[:, cols[j]]))
    nxt = gate_up(0)
    for j in range(n_chunks):
        g, u = nxt
        if j + 1 < n_chunks:
            nxt = gate_up(j + 1)
        a_scr[:, cols[j]] = (_silu(g) * u).astype(BF16)
    y = _mm(a_scr[...], wd_ref[...])
    o_ref[...] = _gated_residual(x, y, mod, ng, k, 0.5)


def _half_ffn(x, mod, ng, wg, wu, wd, k, rows_per_mod, mod_base):
    m, d = x.shape
    d_ff = wg.shape[1]
    tm = _pick_tile(rows_per_mod, (1024, 512, 256, 128))
    tiles_per_mod = rows_per_mod // tm
    return pl.pallas_call(
        functools.partial(_ffn_kernel, k=k),
        grid=(m // tm,),
        in_specs=[
            pl.BlockSpec((tm, d), lambda i: (i, 0)),
            pl.BlockSpec((1, N_MOD, d), lambda i: (mod_base + i // tiles_per_mod, 0, 0)),
            _resident(ng.shape),
            _resident(wg.shape),
            _resident(wu.shape),
            _resident(wd.shape),
        ],
        out_specs=pl.BlockSpec((tm, d), lambda i: (i, 0)),
        out_shape=jax.ShapeDtypeStruct((m, d), F32),
        scratch_shapes=[pltpu.VMEM((tm, d_ff), BF16)],
        compiler_params=_cparams(1),
        name=f"half_ffn{k}",
    )(x, mod, ng, wg, wu, wd)


def _head_rms(x, gain):
    low = lax.broadcasted_iota(jnp.int32, x.shape, 1) < HEAD_DIM
    ss = x * x
    s_lo = jnp.sum(jnp.where(low, ss, 0.0), axis=-1, keepdims=True)
    s_hi = jnp.sum(jnp.where(low, 0.0, ss), axis=-1, keepdims=True)
    inv = jnp.where(low, lax.rsqrt(s_lo * (1.0 / HEAD_DIM) + EPS), lax.rsqrt(s_hi * (1.0 / HEAD_DIM) + EPS))
    return x * inv * gain


def _rope(x, cos, sin_signed):
    lane = lax.broadcasted_iota(jnp.int32, x.shape, 1)
    first_half = (lane & ROPE_PAIRS) == 0
    partner = jnp.where(first_half, pltpu.roll(x, LANES - ROPE_PAIRS, 1), pltpu.roll(x, ROPE_PAIRS, 1))
    return x * cos + partner * sin_signed


CONV_HALO = SUBLANES


def _inproj_kernel(*refs, rope, tiles_per_seq):
    x_ref, prev_ref, next_ref, mod_ref, ng_ref = refs[:5]
    refs = refs[5:]
    if rope:
        cos, sin = refs[0][...], refs[1][...]
        refs = refs[2:]
    (qkn_ref, w_qa, w_kva, w_qb, w_kvb, w_qkm, w_vm, w_om, w_g, w_gt, cw_ref, cb_ref,
     qa_ref, ka_ref, va_ref, qb_ref, kb_ref, vb_ref, qmt_ref, km_ref, vm_ref, om_ref, g_ref, gt_ref, xp) = refs
    tm = x_ref.shape[0]
    mod, ng = mod_ref[0], ng_ref[...]
    h = _mod_norm(x_ref[...], mod, ng, 1).astype(BF16)
    qn, kn = qkn_ref[0:1], qkn_ref[1:2]
    kv_w = 2 * N_KV * HEAD_DIM

    def finish(full, out_ref, width, norm_gain, scale):
        for j in range(width // LANES):
            blk = full[:, j * LANES:(j + 1) * LANES]
            if norm_gain is not None:
                blk = _head_rms(blk, norm_gain)
            if rope:
                blk = _rope(blk, cos, sin)
            if scale != 1.0:
                blk = blk * scale
            out_ref[:, j * LANES:(j + 1) * LANES] = blk.astype(BF16)

    q_scale = HEAD_DIM ** -0.5 * LOG2E
    halo = jnp.concatenate([prev_ref[...], next_ref[...]], axis=0)
    h_halo = _mod_norm(halo, mod, ng, 1).astype(BF16)
    qa = _mm(h, w_qa[...])
    kva = _mm(h, w_kva[...])
    finish(qa, qa_ref, ATT_WIDTH, None, q_scale)
    qb = _mm(h, w_qb[...])
    finish(kva, ka_ref, kv_w, None, 1.0)
    va_ref[...] = kva[:, kv_w:].astype(BF16)
    kvb = _mm(h, w_kvb[...])
    finish(qb, qb_ref, ATT_WIDTH, qn, q_scale)
    qk = _mm(jnp.concatenate([h, h_halo], axis=0), w_qkm[...])
    finish(kvb, kb_ref, kv_w, kn, 1.0)
    vb_ref[...] = kvb[:, kv_w:].astype(BF16)
    vm_ref[...] = _mm_nt(w_vm[...], h).astype(BF16)
    om_ref[...] = _mm(h, w_om[...])
    g_ref[...] = _mm(h, w_g[...])
    gt_ref[0] = _mm_nt(w_gt[...], h)

    ti = pl.program_id(0) % tiles_per_seq
    zero = jnp.zeros((CONV_HALO, qk.shape[1]), F32)
    xp[0:CONV_HALO] = jnp.where(ti > 0, qk[tm:tm + CONV_HALO], zero)
    xp[CONV_HALO:CONV_HALO + tm] = qk[:tm]
    xp[CONV_HALO + tm:] = jnp.where(ti < tiles_per_seq - 1, qk[tm + CONV_HALO:], zero)
    first = CONV_HALO - C_CONV // 2
    acc = cb_ref[...] + cw_ref[0:1] * xp[first:first + tm]
    for j in range(1, C_CONV):
        acc = acc + cw_ref[j:j + 1] * xp[first + j:first + j + tm]
    y = _silu(acc)
    qmt_ref[...] = (y[:, :C_WIDTH] * (C_HEAD_DIM ** -0.5)).T.astype(BF16)
    km_ref[...] = y[:, C_WIDTH:].astype(BF16)


def _in_projection(x, mod, ng, rope_tabs, qkn, weights, conv_w, conv_b, rows_per_mod, mod_base, t_seq):
    m, d = x.shape
    tm = _pick_tile(min(rows_per_mod, t_seq))
    tiles_per_mod = rows_per_mod // tm
    tiles_per_seq = t_seq // tm
    rope = rope_tabs is not None
    row = lambda i: (i, 0)
    hb = tm // CONV_HALO
    n_hb = m // CONV_HALO
    in_specs = [
        pl.BlockSpec((tm, d), row),
        pl.BlockSpec((CONV_HALO, d), lambda i: (jnp.maximum(i * hb - 1, 0), 0)),
        pl.BlockSpec((CONV_HALO, d), lambda i: (jnp.minimum((i + 1) * hb, n_hb - 1), 0)),
        pl.BlockSpec((1, N_MOD, d), lambda i: (mod_base + i // tiles_per_mod, 0, 0)),
        _resident(ng.shape),
    ]
    args = [x, x, x, mod, ng]
    if rope:
        in_specs += [pl.BlockSpec((tm, LANES), lambda i: (i % tiles_per_seq, 0))] * 2
        args += list(rope_tabs)
    in_specs += [_resident(qkn.shape)] + [_resident(w.shape) for w in weights]
    in_specs += [_resident(conv_w.shape), _resident((1, conv_w.shape[1]))]
    args += [qkn] + list(weights) + [conv_w, conv_b.reshape(1, -1)]
    kv_w = 2 * N_KV * HEAD_DIM
    col = lambda i: (0, i)
    outs = [(ATT_WIDTH, BF16, False), (kv_w, BF16, False), (kv_w, BF16, False),
            (ATT_WIDTH, BF16, False), (kv_w, BF16, False), (kv_w, BF16, False),
            (C_WIDTH, BF16, True), (C_WIDTH, BF16, False), (C_WIDTH, BF16, True), (C_WIDTH, F32, False),
            (N_GATES, F32, False)]
    out_specs = [pl.BlockSpec((w, tm), col) if fm else pl.BlockSpec((tm, w), row) for w, _, fm in outs]
    out_shape = [jax.ShapeDtypeStruct((w, m) if fm else (m, w), dt) for w, dt, fm in outs]
    out_specs.append(pl.BlockSpec((1, N_GATES, tm), lambda i: (i // tiles_per_seq, 0, i % tiles_per_seq)))
    out_shape.append(jax.ShapeDtypeStruct((m // t_seq, N_GATES, t_seq), F32))
    return pl.pallas_call(
        functools.partial(_inproj_kernel, rope=rope, tiles_per_seq=tiles_per_seq),
        grid=(m // tm,),
        in_specs=in_specs,
        out_specs=out_specs,
        out_shape=out_shape,
        scratch_shapes=[pltpu.VMEM((tm + 2 * CONV_HALO, 2 * C_WIDTH), F32)],
        compiler_params=_cparams(1),
        name="in_proj_rope" if rope else "in_proj",
    )(*args)


DENSE_SMALL_KEYS = 1024
SCORE_LOOKAHEAD = 1


def _attend(q_ref, rows, k, v, bias, sink_ref, o_ref):
    nr = rows.stop - rows.start
    nk = k.shape[0]
    gsz = N_HEADS // N_KV
    hpp = gsz if nk <= DENSE_SMALL_KEYS else 1
    upper = lax.broadcasted_iota(jnp.int32, (nr, LANES), 1) >= HEAD_DIM

    def scores(first):
        grp = first // gsz
        qs = []
        for hd in range(first, first + hpp):
            qblk = q_ref[rows, (hd // 2) * LANES:(hd // 2 + 1) * LANES]
            mine = upper if hd % 2 else jnp.logical_not(upper)
            qs.append(jnp.where(mine, qblk, jnp.zeros_like(qblk)))
        q_stack = qs[0] if hpp == 1 else jnp.concatenate(qs, axis=0)
        s = _mm_nt(q_stack, k[:, grp * LANES:(grp + 1) * LANES]).reshape(hpp, nr, nk)
        return s if bias is None else s + bias[None]

    def softmax(first, s):
        mx = jnp.max(s, axis=-1, keepdims=True)
        if sink_ref is not None:
            hidx = lax.broadcasted_iota(jnp.int32, (hpp, 1, 1), 0)
            sink = jnp.zeros((hpp, 1, 1), F32)
            for i in range(hpp):
                sink = jnp.where(hidx == i, sink_ref[0, first + i] * LOG2E, sink)
            mx = jnp.maximum(mx, sink)
        p = jnp.exp2(s - mx)
        den = jnp.sum(p, axis=-1, keepdims=True)
        if sink_ref is not None:
            den = den + jnp.exp2(sink - mx)
        return p.astype(BF16).reshape(hpp * nr, nk), den

    def weighted_values(first, p, den):
        grp = first // gsz
        o = _mm(p, v[:, grp * LANES:(grp + 1) * LANES]).reshape(hpp, nr, LANES) / den
        return [o[i] for i in range(hpp)]

    passes = list(range(0, N_HEADS, hpp))
    n_pass = len(passes)
    outs = []
    s_buf = {i: scores(passes[i]) for i in range(min(SCORE_LOOKAHEAD, n_pass))}
    for i in range(n_pass):
        if i + SCORE_LOOKAHEAD < n_pass:
            s_buf[i + SCORE_LOOKAHEAD] = scores(passes[i + SCORE_LOOKAHEAD])
        outs += weighted_values(passes[i], *softmax(passes[i], s_buf.pop(i)))
    for j in range(N_HEADS // 2):
        o_ref[rows, j * LANES:(j + 1) * LANES] = jnp.where(upper, outs[2 * j + 1], outs[2 * j]).astype(BF16)


def _window_attn_kernel(*refs, n_kblk, t_seq):
    q_ref, kc_ref, vc_ref = refs[:3]
    kw = refs[3:3 + n_kblk]
    vw = refs[3 + n_kblk:3 + 2 * n_kblk]
    sink_ref, o_ref = refs[3 + 2 * n_kblk:]
    qb = q_ref.shape[0]
    t_ctx = kc_ref.shape[0]
    k = jnp.concatenate([kc_ref[...]] + [r[...] for r in kw], axis=0)
    v = jnp.concatenate([vc_ref[...]] + [r[...] for r in vw], axis=0)
    nk = k.shape[0]
    row = lax.broadcasted_iota(jnp.int32, (qb, nk), 0)
    col = lax.broadcasted_iota(jnp.int32, (qb, nk), 1)
    rel = col - t_ctx - WINDOW
    pos = rel + pl.program_id(1) * qb
    visible = (col < t_ctx) | ((jnp.abs(rel - row) <= WINDOW) & (pos >= 0) & (pos < t_seq))
    bias = jnp.where(visible, 0.0, -jnp.inf)
    _attend(q_ref, slice(0, qb), k, v, bias, sink_ref, o_ref)


def _window_attention(q, k, v, k_ctx, v_ctx, sink, bsz, t_seq, t_ctx):
    m = q.shape[0]
    qb = _pick_tile(t_seq, (256, 128))
    nq = t_seq // qb
    kblk_per_q = qb // WINDOW
    n_kblk = kblk_per_q + 2
    nkb_seq = t_seq // WINDOW
    kvw = k.shape[1]

    def win_spec(j):
        def idx(b, i):
            blk = jnp.clip(i * kblk_per_q + (j - 1), 0, nkb_seq - 1)
            return (b * nkb_seq + blk, 0)
        return pl.BlockSpec((WINDOW, kvw), idx)

    ctx_spec = pl.BlockSpec((t_ctx, kvw), lambda b, i: (b, 0))
    in_specs = ([pl.BlockSpec((qb, ATT_WIDTH), lambda b, i: (b * nq + i, 0)), ctx_spec, ctx_spec]
                + [win_spec(j) for j in range(n_kblk)] * 2
                + [pl.BlockSpec(memory_space=pltpu.SMEM)])
    return pl.pallas_call(
        functools.partial(_window_attn_kernel, n_kblk=n_kblk, t_seq=t_seq),
        grid=(bsz, nq),
        in_specs=in_specs,
        out_specs=pl.BlockSpec((qb, ATT_WIDTH), lambda b, i: (b * nq + i, 0)),
        out_shape=jax.ShapeDtypeStruct((m, ATT_WIDTH), BF16),
        compiler_params=_cparams(2),
        name="window_attn",
    )(q, k_ctx, v_ctx, *([k] * n_kblk), *([v] * n_kblk), sink)


def _dense_attn_kernel(*refs, n_kv_src, has_sink):
    q_ref = refs[0]
    k_refs = refs[1:1 + n_kv_src]
    v_refs = refs[1 + n_kv_src:1 + 2 * n_kv_src]
    rest = refs[1 + 2 * n_kv_src:]
    sink_ref = rest[0] if has_sink else None
    o_ref = rest[-1]
    if n_kv_src == 1:
        k, v = k_refs[0][...], v_refs[0][...]
    else:
        k = jnp.concatenate([r[...] for r in k_refs], axis=0)
        v = jnp.concatenate([r[...] for r in v_refs], axis=0)
    _attend(q_ref, slice(0, q_ref.shape[0]), k, v, None, sink_ref, o_ref)


def _dense_attention(q, ks, vs, sink, bsz, t_q):
    m = q.shape[0]
    qb = _pick_tile(t_q, (512, 256, 128))
    nq = t_q // qb
    kv_specs = [pl.BlockSpec((a.shape[0] // bsz, a.shape[1]), lambda b, i: (b, 0)) for a in ks]
    in_specs = [pl.BlockSpec((qb, ATT_WIDTH), lambda b, i: (b * nq + i, 0))] + kv_specs * 2
    args = [q, *ks, *vs]
    if sink is not None:
        in_specs.append(pl.BlockSpec(memory_space=pltpu.SMEM))
        args.append(sink)
    return pl.pallas_call(
        functools.partial(_dense_attn_kernel, n_kv_src=len(ks), has_sink=sink is not None),
        grid=(bsz, nq),
        in_specs=in_specs,
        out_specs=pl.BlockSpec((qb, ATT_WIDTH), lambda b, i: (b * nq + i, 0)),
        out_shape=jax.ShapeDtypeStruct((m, ATT_WIDTH), BF16),
        compiler_params=_cparams(2),
        name="dense_attn",
    )(*args)


def _split3(x):
    hi = x.astype(BF16)
    r1 = x - hi.astype(F32)
    mid = r1.astype(BF16)
    lo = (r1 - mid.astype(F32)).astype(BF16)
    return hi, mid, lo


GATE_QUANTITIES = 5
GQ_A, GQ_INTER, GQ_NRM, GQ_END, GQ_DECAY = range(GATE_QUANTITIES)


def _mlstm_gate_kernel(gt_ref, gbt_ref, m0_ref, o_ref, m_out, m_scr, *, rev):
    gi = pl.program_id(0)
    n_g = pl.num_programs(0)
    bsz, _, blk = gt_ref.shape
    rows = bsz * N_GATES
    n_chunks = blk // CHUNK

    @pl.when(gi == 0)
    def _():
        m_scr[...] = m0_ref[...]

    gates_t = (gt_ref[...] + gbt_ref[...]).reshape(rows, blk)
    r = lax.broadcasted_iota(jnp.int32, (blk, blk), 0)
    c = lax.broadcasted_iota(jnp.int32, (blk, blk), 1)
    feeds_b = jnp.where((r >= c) if rev else (r <= c), 1.0, 0.0).astype(BF16)
    brow = sum(_mm(piece, feeds_b) for piece in _split3(_log_sigmoid(gates_t)))
    b_all = pltpu.roll(brow, rows - C_HEADS, 0)
    rp = gates_t - b_all
    lane = lax.broadcasted_iota(jnp.int32, (rows, blk), 1)
    pos = lane % CHUNK
    seg = rp
    shift = 1
    while shift < CHUNK:
        if rev:
            seg = jnp.where(pos < CHUNK - shift, jnp.maximum(seg, pltpu.roll(seg, blk - shift, 1)), seg)
        else:
            seg = jnp.where(pos >= shift, jnp.maximum(seg, pltpu.roll(seg, shift, 1)), seg)
        shift *= 2
    m_init = m_scr[:, 0:1]
    m_prev = m_init
    b_off = jnp.zeros((rows, 1), F32)
    m_lane = jnp.zeros((rows, blk), F32)
    off_lane = jnp.zeros((rows, blk), F32)
    for ch in (range(n_chunks - 1, -1, -1) if rev else range(n_chunks)):
        end = ch * CHUNK if rev else (ch + 1) * CHUNK - 1
        here = (lane // CHUNK) == ch
        m_lane = jnp.where(here, m_prev, m_lane)
        off_lane = jnp.where(here, b_off, off_lane)
        b_end = b_all[:, end:end + 1]
        m_prev = (b_end - b_off) + jnp.maximum(m_prev, b_off + seg[:, end:end + 1])
        b_off = b_end
    m_last, b_last = m_prev, b_off
    a = off_lane - jnp.maximum(m_lane, off_lane + seg)
    quantities = [None] * GATE_QUANTITIES
    quantities[GQ_A] = a
    quantities[GQ_INTER] = jnp.exp(a + m_init)
    quantities[GQ_NRM] = jnp.exp(a - b_all)
    quantities[GQ_END] = jnp.exp(rp + (b_last - m_last))
    quantities[GQ_DECAY] = jnp.broadcast_to(jnp.exp(b_last + m_init - m_last), (rows, blk))
    o_ref[...] = jnp.concatenate([x.reshape(bsz, N_GATES, blk) for x in quantities], axis=1)
    m_scr[...] = jnp.broadcast_to(m_last, (rows, LANES))

    @pl.when(gi == n_g - 1)
    def _():
        m_out[...] = m_scr[...]


def _mlstm_gates(gt, gate_b, m_state, t_seq, rev):
    bsz = gt.shape[0]
    blk = _pick_tile(t_seq, (256, 128))
    ng = t_seq // blk
    tok = (lambda i: (0, 0, ng - 1 - i)) if rev else (lambda i: (0, 0, i))
    return pl.pallas_call(
        functools.partial(_mlstm_gate_kernel, rev=rev),
        grid=(ng,),
        in_specs=[pl.BlockSpec((bsz, N_GATES, blk), tok), _resident((1, N_GATES, 1)), _resident(m_state.shape)],
        out_specs=[pl.BlockSpec((bsz, GATE_QUANTITIES * N_GATES, blk), tok), _resident(m_state.shape)],
        out_shape=[jax.ShapeDtypeStruct((bsz, GATE_QUANTITIES * N_GATES, t_seq), F32),
                   jax.ShapeDtypeStruct(m_state.shape, F32)],
        scratch_shapes=[pltpu.VMEM(m_state.shape, F32)],
        compiler_params=_cparams(1),
        name="mlstm_gates_bwd" if rev else "mlstm_gates_fwd",
    )(gt, gate_b.reshape(1, N_GATES, 1), m_state)


N_DIR = 2


def _mlstm_kernel(*refs):
    per_dir = [refs[0:6], refs[6:12]]
    gb_ref = refs[12]
    h_refs, s_outs, s_scr = refs[13:15], refs[15:17], refs[17]
    gi = pl.program_id(1)
    n_g = pl.num_programs(1)
    blk = refs[0].shape[0]

    @pl.when(gi == 0)
    def _():
        for d in range(N_DIR):
            s_scr[d] = per_dir[d][5][0]

    r = lax.broadcasted_iota(jnp.int32, (blk, blk), 0)
    c = lax.broadcasted_iota(jnp.int32, (blk, blk), 1)
    ones = jnp.ones((C_HEAD_DIM, blk), BF16)
    units = []
    for d, rev in enumerate((False, True)):
        k_ref, qt_ref, vt_ref, g_ref, gq_ref, _ = per_dir[d]
        gates = g_ref[...] + gb_ref[...]
        feeds = (r >= c) if rev else (r <= c)
        feeds_tb = jnp.where((c >= r) if rev else (c <= r), 1.0, 0.0).astype(BF16)
        bcol = sum(_mm(feeds_tb, piece) for piece in _split3(_log_sigmoid(gates)))
        base = 2 * C_HEADS if rev else 0
        rp_col = gates[:, base:base + C_HEADS] - bcol[:, base + C_HEADS:base + 2 * C_HEADS]
        gq = gq_ref[0]
        for hd in range(C_HEADS):
            lanes = slice(hd * C_HEAD_DIM, (hd + 1) * C_HEAD_DIM)
            rows = [gq[q * N_GATES + base + hd:q * N_GATES + base + hd + 1] for q in range(GATE_QUANTITIES)]
            units.append(dict(
                d=d, hd=hd, lanes=lanes, feeds=feeds, rows=rows, rp=rp_col[:, hd:hd + 1],
                kh=k_ref[:, lanes],
                qth=qt_ref[lanes, :],
                v_aug=jnp.concatenate([vt_ref[lanes, :], ones], axis=0),
                st=s_scr[d, hd]))
    for u in units:
        u["s"] = _mm(u["kh"], u["qth"])
        u["inter"] = _mm(u["st"].astype(BF16), u["qth"])
        x_t = (u["rows"][GQ_END] * u["v_aug"].astype(F32)).astype(BF16)
        u["st_new"] = u["rows"][GQ_DECAY][:, :C_HEAD_DIM] * u["st"] + _mm(x_t, u["kh"])
    for u in units:
        w_t = jnp.exp(jnp.where(u["feeds"], u["rp"] + u["rows"][GQ_A], -jnp.inf))
        u["p"] = (u["s"] * w_t).astype(BF16)
    for u in units:
        tot = _mm(u["v_aug"], u["p"]) + u["rows"][GQ_INTER] * u["inter"]
        u["h_t"] = tot[:C_HEAD_DIM] / jnp.maximum(jnp.abs(tot[C_HEAD_DIM:]), u["rows"][GQ_NRM])
    for u in units:
        h_refs[u["d"]][:, u["lanes"]] = u["h_t"].T
        s_scr[u["d"], u["hd"]] = u["st_new"]

    @pl.when(gi == n_g - 1)
    def _():
        for d in range(N_DIR):
            s_outs[d][0] = s_scr[d]


def _mlstm_scan(k, qt, vt, g, gqs, gate_b, states, bsz, t_seq):
    m = k.shape[0]
    blk = _pick_tile(t_seq, (256, 128))
    ng = t_seq // blk
    st_spec = pl.BlockSpec((1, C_HEADS, 2 * C_HEAD_DIM, C_HEAD_DIM), lambda b, i: (b, 0, 0, 0))
    in_specs, args, h_specs = [], [], []
    for d, step in enumerate((lambda i: i, lambda i: ng - 1 - i)):
        tok = lambda b, i, step=step: (b * ng + step(i), 0)
        tok_t = lambda b, i, step=step: (0, b * ng + step(i))
        in_specs += [
            pl.BlockSpec((blk, C_WIDTH), tok),
            pl.BlockSpec((C_WIDTH, blk), tok_t),
            pl.BlockSpec((C_WIDTH, blk), tok_t),
            pl.BlockSpec((blk, N_GATES), tok),
            pl.BlockSpec((1, GATE_QUANTITIES * N_GATES, blk), lambda b, i, step=step: (b, 0, step(i))),
            st_spec,
        ]
        args += [k, qt, vt, g, gqs[d], states[d]]
        h_specs.append(pl.BlockSpec((blk, C_WIDTH), tok))
    return pl.pallas_call(
        _mlstm_kernel,
        grid=(bsz, ng),
        in_specs=in_specs + [_resident((1, N_GATES))],
        out_specs=h_specs + [st_spec] * N_DIR,
        out_shape=[jax.ShapeDtypeStruct((m, C_WIDTH), F32)] * N_DIR
                  + [jax.ShapeDtypeStruct(states[0].shape, F32)] * N_DIR,
        scratch_shapes=[pltpu.VMEM((N_DIR, C_HEADS, 2 * C_HEAD_DIM, C_HEAD_DIM), F32)],
        compiler_params=_cparams(2),
        name="mlstm_scan",
    )(*args, gate_b.reshape(1, N_GATES))


def _merge_kernel(x_ref, mod_ref, ng_ref, ya_ref, yb_ref, hf_ref, hb_ref, om_ref, hg_ref,
                  w_gl, w_br, w_out, o_ref):
    x = x_ref[...]
    mod = mod_ref[0]
    ng = ng_ref[...]
    h = _mod_norm(x, mod, ng, 1).astype(BF16)
    d = x.shape[1]
    gate_logits = lambda n: _mm(h, w_gl[:, n * d:(n + 1) * d])
    gl = gate_logits(0)
    za = _mm(ya_ref[...], w_br[0])
    hsum = hf_ref[...] + hb_ref[...]
    om = om_ref[...]
    ym = []
    for j in range(C_HEADS):
        lanes = slice(j * C_HEAD_DIM, (j + 1) * C_HEAD_DIM)
        ym.append((_sigmoid(om[:, lanes]) * _rms(hsum[:, lanes], hg_ref[:, lanes])).astype(BF16))
    gl_next = gate_logits(1)
    zb = _mm(yb_ref[...], w_br[1])
    acc = _sigmoid(gl) * za
    gl, gl_next = gl_next, gate_logits(2)
    zm = _mm(jnp.concatenate(ym, axis=1), w_br[2])
    acc = acc + _sigmoid(gl) * zb
    acc = acc + _sigmoid(gl_next) * zm
    y = _mm(acc.astype(BF16), w_out[...])
    o_ref[...] = _gated_residual(x, y, mod, ng, 1, 1.0)


def _merge(x, mod, ng, ya, yb, hf, hb, om, head_g, w_gl, w_br, w_out, rows_per_mod, mod_base):
    m, d = x.shape
    tm = _pick_tile(rows_per_mod)
    tiles_per_mod = rows_per_mod // tm
    row = lambda i: (i, 0)
    return pl.pallas_call(
        _merge_kernel,
        grid=(m // tm,),
        in_specs=[
            pl.BlockSpec((tm, d), row),
            pl.BlockSpec((1, N_MOD, d), lambda i: (mod_base + i // tiles_per_mod, 0, 0)),
            _resident(ng.shape),
            pl.BlockSpec((tm, ATT_WIDTH), row),
            pl.BlockSpec((tm, ATT_WIDTH), row),
            pl.BlockSpec((tm, C_WIDTH), row),
            pl.BlockSpec((tm, C_WIDTH), row),
            pl.BlockSpec((tm, C_WIDTH), row),
            _resident((1, C_WIDTH)),
            _resident(w_gl.shape),
            _resident(w_br.shape),
            _resident(w_out.shape),
        ],
        out_specs=pl.BlockSpec((tm, d), row),
        out_shape=jax.ShapeDtypeStruct((m, d), F32),
        compiler_params=_cparams(1),
        name="merge",
    )(x, mod, ng, ya, yb, hf, hb, om, head_g.reshape(1, C_WIDTH), w_gl, w_br, w_out)


def _rope_tables(t_seq):
    rows = t_seq // GRID_W
    row = jnp.repeat(jnp.arange(rows), GRID_W)
    col = jnp.tile(jnp.arange(GRID_W), rows)
    freqs = ROPE_THETA ** (-jnp.arange(ROPE_PAIRS, dtype=F32) / ROPE_PAIRS)
    ang = jnp.stack([row[:, None] * freqs, col[:, None] * freqs], axis=1)
    cos, sin = jnp.cos(ang), jnp.sin(ang)
    cos_head = jnp.concatenate([cos[:, 0], cos[:, 0], cos[:, 1], cos[:, 1]], axis=-1)
    sin_head = jnp.concatenate([-sin[:, 0], sin[:, 0], -sin[:, 1], sin[:, 1]], axis=-1)
    return jnp.tile(cos_head, (1, 2)), jnp.tile(sin_head, (1, 2))


def _split_in_weights(w_in):
    sizes = (ATT_WIDTH, N_KV * HEAD_DIM, N_KV * HEAD_DIM, ATT_WIDTH, N_KV * HEAD_DIM, N_KV * HEAD_DIM,
             C_WIDTH, C_WIDTH, C_WIDTH, C_WIDTH, N_GATES, N_BRANCH * w_in.shape[0])
    parts, start = [], 0
    for s in sizes:
        parts.append(w_in[:, start:start + s])
        start += s
    qa, ka, va, qb, kb, vb, qm, km, vm, om, gm, gl = parts

    def doubled(w):
        return jnp.concatenate([w[:, :HEAD_DIM], w[:, :HEAD_DIM], w[:, HEAD_DIM:], w[:, HEAD_DIM:]], axis=1)

    bf = lambda w: w.astype(BF16)
    weights = [bf(qa), bf(jnp.concatenate([doubled(ka), doubled(va)], axis=1)),
               bf(qb), bf(jnp.concatenate([doubled(kb), doubled(vb)], axis=1)),
               bf(jnp.concatenate([qm, km], axis=1)), bf(vm.T), bf(om), bf(gm), bf(gm.T)]
    return weights, bf(gl)


def kernel(x, c, ctx, c_ctx, w_ada, b_ada, norm_g, ffn_w_gate, ffn_w_up, ffn_w_down, w_in, attn_sink,
           qk_norm_g, conv_w, conv_b, mlstm_gate_b, mlstm_norm_g, w_branch, w_out):
    bsz, t_lat, d = x.shape
    t_ctx = ctx.shape[1]
    depth = w_ada.shape[0]
    xs = x.reshape(bsz * t_lat, d)
    cs = ctx.reshape(bsz * t_ctx, d)
    mods = _modulation(jnp.concatenate([c, c_ctx[None]], axis=0), w_ada, b_ada)
    rope_tabs = _rope_tables(t_lat)
    zero_state = jnp.zeros((bsz, C_HEADS, 2 * C_HEAD_DIM, C_HEAD_DIM), F32)
    zero_m = jnp.zeros((bsz * N_GATES, LANES), F32)

    for l in range(depth):
        last = l == depth - 1
        mod, ng = mods[l], norm_g[l]
        wg, wu, wd = (w[l].astype(BF16) for w in (ffn_w_gate, ffn_w_up, ffn_w_down))
        in_w, w_gl = _split_in_weights(w_in[l])
        qkn = jnp.tile(qk_norm_g[l], (1, 2))
        sink = attn_sink[l].reshape(1, N_HEADS)
        w_br, w_o = w_branch[l].astype(BF16), w_out[l].astype(BF16)
        lat = dict(rows_per_mod=t_lat, mod_base=0)
        con = dict(rows_per_mod=bsz * t_ctx, mod_base=bsz)

        xs = _half_ffn(xs, mod, ng, wg[0], wu[0], wd[0], 0, **lat)
        cs = _half_ffn(cs, mod, ng, wg[0], wu[0], wd[0], 0, **con)

        (qa, ka, va, qb, kb, vb, qmt, km, vm, om, gm, gmt) = _in_projection(
            xs, mod, ng, rope_tabs, qkn, in_w, conv_w[l], conv_b[l], t_seq=t_lat, **lat)
        (qa_c, ka_c, va_c, qb_c, kb_c, vb_c, qmt_c, km_c, vm_c, om_c, gm_c, gmt_c) = _in_projection(
            cs, mod, ng, None, qkn, in_w, conv_w[l], conv_b[l], t_seq=t_ctx, **con)

        ya = _window_attention(qa, ka, va, ka_c, va_c, sink, bsz, t_lat, t_ctx)
        yb = _dense_attention(qb, [kb_c, kb], [vb_c, vb], None, bsz, t_lat)

        gb = mlstm_gate_b[l]
        gq_cf, m_f = _mlstm_gates(gmt_c, gb, zero_m, t_ctx, False)
        gq_cb, m_b = _mlstm_gates(gmt_c, gb, zero_m, t_ctx, True)
        gq_f, _ = _mlstm_gates(gmt, gb, m_f, t_lat, False)
        gq_b, _ = _mlstm_gates(gmt, gb, m_b, t_lat, True)
        hf_c, hb_c, st_f, st_b = _mlstm_scan(km_c, qmt_c, vm_c, gm_c, (gq_cf, gq_cb), gb,
                                             (zero_state, zero_state), bsz, t_ctx)
        hf, hb, _, _ = _mlstm_scan(km, qmt, vm, gm, (gq_f, gq_b), gb, (st_f, st_b), bsz, t_lat)

        xs = _merge(xs, mod, ng, ya, yb, hf, hb, om, mlstm_norm_g[l], w_gl, w_br, w_o, **lat)
        xs = _half_ffn(xs, mod, ng, wg[1], wu[1], wd[1], 2, **lat)
        if not last:
            ya_c = _dense_attention(qa_c, [ka_c], [va_c], sink, bsz, t_ctx)
            yb_c = _dense_attention(qb_c, [kb_c], [vb_c], None, bsz, t_ctx)
            cs = _merge(cs, mod, ng, ya_c, yb_c, hf_c, hb_c, om_c, mlstm_norm_g[l], w_gl, w_br, w_o, **con)
            cs = _half_ffn(cs, mod, ng, wg[1], wu[1], wd[1], 2, **con)
    return xs.reshape(bsz, t_lat, d)
```

```python
import functools

import jax
import jax.numpy as jnp
from jax import lax
from jax.experimental import pallas as pl
from jax.experimental.pallas import tpu as pltpu

F32 = jnp.float32
BF16 = jnp.bfloat16

EPS = 1e-6
N_MOD = 9
GRID_W = 64
ROPE_THETA = 10000.0
HEAD_DIM = 64
ROPE_PAIRS = HEAD_DIM // 4
N_HEADS = 8
N_KV = 2
ATT_WIDTH = N_HEADS * HEAD_DIM
WINDOW = 128
C_HEADS = 4
C_HEAD_DIM = 128
C_WIDTH = C_HEADS * C_HEAD_DIM
C_CONV = 5
CHUNK = 64
N_GATES = 4 * C_HEADS
N_BRANCH = 3
LOG2E = 1.4426950408889634

LANES = 128
SUBLANES = 8
VMEM_LIMIT_BYTES = 56 * 1024 * 1024


def _cparams(n_axes):
    return pltpu.CompilerParams(
        dimension_semantics=("arbitrary",) * n_axes,
        vmem_limit_bytes=VMEM_LIMIT_BYTES,
    )


def _pick_tile(n, candidates=(512, 256, 128)):
    for c in candidates:
        if n % c == 0:
            return c
    raise ValueError(f"no tile size for {n}")


def _resident(shape):
    zeros = (0,) * len(shape)
    return pl.BlockSpec(shape, lambda *_: zeros, pipeline_mode=pl.Buffered(1))


class _Slab:
    def __init__(self, arr, *idx):
        self.arr, self.idx = arr, idx
        self.shape = arr.shape[len(idx):]


def _resident_slab(slab):
    index = slab.idx + (0,) * len(slab.shape)
    return pl.BlockSpec((None,) * len(slab.idx) + slab.shape, lambda *_: index, pipeline_mode=pl.Buffered(1))


def _mod_spec(mod, row_of_step):
    lead = mod.idx
    return pl.BlockSpec((None,) * len(lead) + (1,) + mod.shape[1:], lambda i: lead + (row_of_step(i), 0, 0))


def _mm(a, b):
    return jnp.dot(a, b, preferred_element_type=F32)


def _mm_nt(a, b):
    return lax.dot_general(a, b, (((1,), (1,)), ((), ())), preferred_element_type=F32)


def _rms(x, g):
    return x * lax.rsqrt(jnp.mean(x * x, axis=-1, keepdims=True) + EPS) * g


def _sigmoid(x):
    return 1.0 / (1.0 + jnp.exp(-x))


def _silu(x):
    return x * _sigmoid(x)


def _log_sigmoid(x):
    return jnp.minimum(x, 0.0) - jnp.log(1.0 + jnp.exp(-jnp.abs(x)))


def _mod_norm(x, mod, ng, k):
    return _rms(x, ng[2 * k:2 * k + 1]) * (1.0 + mod[3 * k + 1:3 * k + 2]) + mod[3 * k:3 * k + 1]


def _gated_residual(x, y, mod, ng, k, weight):
    return x + (weight * mod[3 * k + 2:3 * k + 3]) * _rms(y, ng[2 * k + 1:2 * k + 2])


def _mod_kernel(c_ref, w_ref, b_ref, o_ref):
    a = _silu(c_ref[...]).astype(BF16)
    o_ref[0] = _mm(a, w_ref[0].astype(BF16)) + b_ref[0]


def _modulation(c_all, w_ada, b_ada):
    n_layers, d, nd = w_ada.shape
    r = c_all.shape[0]
    out = pl.pallas_call(
        _mod_kernel,
        grid=(n_layers, nd // d),
        in_specs=[
            pl.BlockSpec((r, d), lambda l, j: (0, 0)),
            pl.BlockSpec((1, d, d), lambda l, j: (l, 0, j)),
            pl.BlockSpec((1, 1, d), lambda l, j: (l, 0, j)),
        ],
        out_specs=pl.BlockSpec((1, r, d), lambda l, j: (l, 0, j)),
        out_shape=jax.ShapeDtypeStruct((n_layers, r, nd), F32),
        compiler_params=_cparams(2),
        name="adaln_mod",
    )(c_all, w_ada, b_ada.reshape(n_layers, 1, nd))
    return out.reshape(n_layers, r, N_MOD, d)


FFN_COL_CHUNK = 256


def _ffn_kernel(x_ref, mod_ref, ng_ref, wg_ref, wu_ref, wd_ref, o_ref, a_scr, *, k):
    x = x_ref[...]
    mod = mod_ref[0]
    ng = ng_ref[...]
    h = _mod_norm(x, mod, ng, k).astype(BF16)
    n_chunks = wg_ref.shape[1] // FFN_COL_CHUNK
    cols = [slice(j * FFN_COL_CHUNK, (j + 1) * FFN_COL_CHUNK) for j in range(n_chunks)]
    gate_up = lambda j: (_mm(h, wg_ref[:, cols[j]]), _mm(h, wu_ref[:, cols[j]]))
    nxt = gate_up(0)
    for j in range(n_chunks):
        g, u = nxt
        if j + 1 < n_chunks:
            nxt = gate_up(j + 1)
        a_scr[:, cols[j]] = (_silu(g) * u).astype(BF16)
    y = _mm(a_scr[...], wd_ref[...])
    o_ref[...] = _gated_residual(x, y, mod, ng, k, 0.5)


def _half_ffn(x, mod, ng, wg, wu, wd, k, rows_per_mod, mod_base):
    m, d = x.shape
    d_ff = wg.shape[1]
    tm = _pick_tile(rows_per_mod, (1024, 512, 256, 128))
    tiles_per_mod = rows_per_mod // tm
    slabs = [ng, wg, wu, wd]
    return pl.pallas_call(
        functools.partial(_ffn_kernel, k=k),
        grid=(m // tm,),
        in_specs=[pl.BlockSpec((tm, d), lambda i: (i, 0)),
                  _mod_spec(mod, lambda i: mod_base + i // tiles_per_mod)] + [_resident_slab(s) for s in slabs],
        out_specs=pl.BlockSpec((tm, d), lambda i: (i, 0)),
        out_shape=jax.ShapeDtypeStruct((m, d), F32),
        scratch_shapes=[pltpu.VMEM((tm, d_ff), BF16)],
        compiler_params=_cparams(1),
        name=f"half_ffn{k}",
    )(x, mod.arr, *[s.arr for s in slabs])


def _head_rms(x, gain):
    low = lax.broadcasted_iota(jnp.int32, x.shape, 1) < HEAD_DIM
    ss = x * x
    s_lo = jnp.sum(jnp.where(low, ss, 0.0), axis=-1, keepdims=True)
    s_hi = jnp.sum(jnp.where(low, 0.0, ss), axis=-1, keepdims=True)
    inv = jnp.where(low, lax.rsqrt(s_lo * (1.0 / HEAD_DIM) + EPS), lax.rsqrt(s_hi * (1.0 / HEAD_DIM) + EPS))
    return x * inv * gain


def _rope(x, cos, sin_signed):
    lane = lax.broadcasted_iota(jnp.int32, x.shape, 1)
    first_half = (lane & ROPE_PAIRS) == 0
    partner = jnp.where(first_half, pltpu.roll(x, LANES - ROPE_PAIRS, 1), pltpu.roll(x, ROPE_PAIRS, 1))
    return x * cos + partner * sin_signed


CONV_HALO = SUBLANES


def _inproj_kernel(*refs, rope, tiles_per_seq):
    x_ref, prev_ref, next_ref, mod_ref, ng_ref = refs[:5]
    refs = refs[5:]
    if rope:
        cos, sin = refs[0][...], refs[1][...]
        refs = refs[2:]
    (qkn_ref, w_qa, w_kva, w_qb, w_kvb, w_qkm, w_vgt, w_om, w_g, cw_ref, cb_ref,
     qa_ref, ka_ref, va_ref, qb_ref, kb_ref, vb_ref, qmt_ref, km_ref, vm_ref, om_ref, g_ref, gt_ref, xp) = refs
    tm = x_ref.shape[0]
    mod, ng = mod_ref[0], ng_ref[...]
    h = _mod_norm(x_ref[...], mod, ng, 1).astype(BF16)
    qn, kn = qkn_ref[0:1], qkn_ref[1:2]
    kv_w = 2 * N_KV * HEAD_DIM

    def finish(full, out_ref, width, norm_gain, scale):
        for j in range(width // LANES):
            blk = full[:, j * LANES:(j + 1) * LANES]
            if norm_gain is not None:
                blk = _head_rms(blk, norm_gain)
            if rope:
                blk = _rope(blk, cos, sin)
            if scale != 1.0:
                blk = blk * scale
            out_ref[:, j * LANES:(j + 1) * LANES] = blk.astype(BF16)

    q_scale = HEAD_DIM ** -0.5 * LOG2E
    halo = jnp.concatenate([prev_ref[...], next_ref[...]], axis=0)
    h_halo = _mod_norm(halo, mod, ng, 1).astype(BF16)
    qa = _mm(h, w_qa[...])
    kva = _mm(h, w_kva[...])
    finish(qa, qa_ref, ATT_WIDTH, None, q_scale)
    qb = _mm(h, w_qb[...])
    finish(kva, ka_ref, kv_w, None, 1.0)
    va_ref[...] = kva[:, kv_w:].astype(BF16)
    kvb = _mm(h, w_kvb[...])
    finish(qb, qb_ref, ATT_WIDTH, qn, q_scale)
    qk = _mm(jnp.concatenate([h, h_halo], axis=0), w_qkm[...])
    finish(kvb, kb_ref, kv_w, kn, 1.0)
    vb_ref[...] = kvb[:, kv_w:].astype(BF16)
    vgt = _mm_nt(w_vgt[...], h)
    vm_ref[...] = vgt[:C_WIDTH].astype(BF16)
    gt_ref[0] = vgt[C_WIDTH:]
    om_ref[...] = _mm(h, w_om[...])
    g_ref[...] = _mm(h, w_g[...])

    ti = pl.program_id(0) % tiles_per_seq
    zero = jnp.zeros((CONV_HALO, qk.shape[1]), F32)
    xp[0:CONV_HALO] = jnp.where(ti > 0, qk[tm:tm + CONV_HALO], zero)
    xp[CONV_HALO:CONV_HALO + tm] = qk[:tm]
    xp[CONV_HALO + tm:] = jnp.where(ti < tiles_per_seq - 1, qk[tm + CONV_HALO:], zero)
    first = CONV_HALO - C_CONV // 2
    acc = cb_ref[...] + cw_ref[0:1] * xp[first:first + tm]
    for j in range(1, C_CONV):
        acc = acc + cw_ref[j:j + 1] * xp[first + j:first + j + tm]
    y = _silu(acc)
    qmt_ref[...] = (y[:, :C_WIDTH] * (C_HEAD_DIM ** -0.5)).T.astype(BF16)
    km_ref[...] = y[:, C_WIDTH:].astype(BF16)


def _in_projection(x, mod, ng, rope_tabs, qkn, weights, conv_w, conv_b, rows_per_mod, mod_base, t_seq):
    m, d = x.shape
    tm = _pick_tile(min(rows_per_mod, t_seq))
    tiles_per_mod = rows_per_mod // tm
    tiles_per_seq = t_seq // tm
    rope = rope_tabs is not None
    row = lambda i: (i, 0)
    hb = tm // CONV_HALO
    n_hb = m // CONV_HALO
    in_specs = [
        pl.BlockSpec((tm, d), row),
        pl.BlockSpec((CONV_HALO, d), lambda i: (jnp.maximum(i * hb - 1, 0), 0)),
        pl.BlockSpec((CONV_HALO, d), lambda i: (jnp.minimum((i + 1) * hb, n_hb - 1), 0)),
        _mod_spec(mod, lambda i: mod_base + i // tiles_per_mod),
        _resident_slab(ng),
    ]
    args = [x, x, x, mod.arr, ng.arr]
    if rope:
        in_specs += [pl.BlockSpec((tm, LANES), lambda i: (i % tiles_per_seq, 0))] * 2
        args += list(rope_tabs)
    slabs = [qkn] + list(weights) + [conv_w, conv_b]
    in_specs += [_resident_slab(s) for s in slabs]
    args += [s.arr for s in slabs]
    kv_w = 2 * N_KV * HEAD_DIM
    col = lambda i: (0, i)
    outs = [(ATT_WIDTH, BF16, False), (kv_w, BF16, False), (kv_w, BF16, False),
            (ATT_WIDTH, BF16, False), (kv_w, BF16, False), (kv_w, BF16, False),
            (C_WIDTH, BF16, True), (C_WIDTH, BF16, False), (C_WIDTH, BF16, True), (C_WIDTH, F32, False),
            (N_GATES, F32, False)]
    out_specs = [pl.BlockSpec((w, tm), col) if fm else pl.BlockSpec((tm, w), row) for w, _, fm in outs]
    out_shape = [jax.ShapeDtypeStruct((w, m) if fm else (m, w), dt) for w, dt, fm in outs]
    out_specs.append(pl.BlockSpec((1, N_GATES, tm), lambda i: (i // tiles_per_seq, 0, i % tiles_per_seq)))
    out_shape.append(jax.ShapeDtypeStruct((m // t_seq, N_GATES, t_seq), F32))
    return pl.pallas_call(
        functools.partial(_inproj_kernel, rope=rope, tiles_per_seq=tiles_per_seq),
        grid=(m // tm,),
        in_specs=in_specs,
        out_specs=out_specs,
        out_shape=out_shape,
        scratch_shapes=[pltpu.VMEM((tm + 2 * CONV_HALO, 2 * C_WIDTH), F32)],
        compiler_params=_cparams(1),
        name="in_proj_rope" if rope else "in_proj",
    )(*args)


DENSE_SMALL_KEYS = 1024
SCORE_LOOKAHEAD = 1


def _attend(q_ref, rows, k, v, bias, sink_ref, o_ref):
    nr = rows.stop - rows.start
    nk = k.shape[0]
    gsz = N_HEADS // N_KV
    hpp = 2 if nk <= DENSE_SMALL_KEYS else 1
    upper = lax.broadcasted_iota(jnp.int32, (nr, LANES), 1) >= HEAD_DIM

    def scores(first):
        grp = first // gsz
        qs = []
        for hd in range(first, first + hpp):
            qblk = q_ref[rows, (hd // 2) * LANES:(hd // 2 + 1) * LANES]
            mine = upper if hd % 2 else jnp.logical_not(upper)
            qs.append(jnp.where(mine, qblk, jnp.zeros_like(qblk)))
        q_stack = qs[0] if hpp == 1 else jnp.concatenate(qs, axis=0)
        s = _mm_nt(q_stack, k[:, grp * LANES:(grp + 1) * LANES]).reshape(hpp, nr, nk)
        return s if bias is None else s + bias[None]

    def softmax(first, s):
        mx = jnp.max(s, axis=-1, keepdims=True)
        if sink_ref is not None:
            hidx = lax.broadcasted_iota(jnp.int32, (hpp, 1, 1), 0)
            sink = jnp.zeros((hpp, 1, 1), F32)
            for i in range(hpp):
                sink = jnp.where(hidx == i, sink_ref[0, first + i] * LOG2E, sink)
            mx = jnp.maximum(mx, sink)
        p = jnp.exp2(s - mx)
        den = jnp.sum(p, axis=-1, keepdims=True)
        if sink_ref is not None:
            den = den + jnp.exp2(sink - mx)
        return p.astype(BF16).reshape(hpp * nr, nk), den

    def weighted_values(first, p, den):
        grp = first // gsz
        o = _mm(p, v[:, grp * LANES:(grp + 1) * LANES]).reshape(hpp, nr, LANES) / den
        return [o[i] for i in range(hpp)]

    passes = list(range(0, N_HEADS, hpp))
    n_pass = len(passes)
    outs = []
    s_buf = {i: scores(passes[i]) for i in range(min(SCORE_LOOKAHEAD, n_pass))}
    for i in range(n_pass):
        if i + SCORE_LOOKAHEAD < n_pass:
            s_buf[i + SCORE_LOOKAHEAD] = scores(passes[i + SCORE_LOOKAHEAD])
        outs += weighted_values(passes[i], *softmax(passes[i], s_buf.pop(i)))
    for j in range(N_HEADS // 2):
        o_ref[rows, j * LANES:(j + 1) * LANES] = jnp.where(upper, outs[2 * j + 1], outs[2 * j]).astype(BF16)


def _window_attn_kernel(*refs, n_kblk, t_seq):
    q_ref, kc_ref, vc_ref = refs[:3]
    kw = refs[3:3 + n_kblk]
    vw = refs[3 + n_kblk:3 + 2 * n_kblk]
    sink_ref, o_ref = refs[3 + 2 * n_kblk:]
    qb = q_ref.shape[0]
    t_ctx = kc_ref.shape[0]
    k = jnp.concatenate([kc_ref[...]] + [r[...] for r in kw], axis=0)
    v = jnp.concatenate([vc_ref[...]] + [r[...] for r in vw], axis=0)
    nk = k.shape[0]
    row = lax.broadcasted_iota(jnp.int32, (qb, nk), 0)
    col = lax.broadcasted_iota(jnp.int32, (qb, nk), 1)
    rel = col - t_ctx - WINDOW
    pos = rel + pl.program_id(1) * qb
    visible = (col < t_ctx) | ((jnp.abs(rel - row) <= WINDOW) & (pos >= 0) & (pos < t_seq))
    bias = jnp.where(visible, 0.0, -jnp.inf)
    _attend(q_ref, slice(0, qb), k, v, bias, sink_ref, o_ref)


def _window_attention(q, k, v, k_ctx, v_ctx, sink, bsz, t_seq, t_ctx):
    m = q.shape[0]
    qb = _pick_tile(t_seq, (256, 128))
    nq = t_seq // qb
    kblk_per_q = qb // WINDOW
    n_kblk = kblk_per_q + 2
    nkb_seq = t_seq // WINDOW
    kvw = k.shape[1]

    def win_spec(j):
        def idx(b, i):
            blk = jnp.clip(i * kblk_per_q + (j - 1), 0, nkb_seq - 1)
            return (b * nkb_seq + blk, 0)
        return pl.BlockSpec((WINDOW, kvw), idx)

    ctx_spec = pl.BlockSpec((t_ctx, kvw), lambda b, i: (b, 0))
    in_specs = ([pl.BlockSpec((qb, ATT_WIDTH), lambda b, i: (b * nq + i, 0)), ctx_spec, ctx_spec]
                + [win_spec(j) for j in range(n_kblk)] * 2
                + [pl.BlockSpec(memory_space=pltpu.SMEM)])
    return pl.pallas_call(
        functools.partial(_window_attn_kernel, n_kblk=n_kblk, t_seq=t_seq),
        grid=(bsz, nq),
        in_specs=in_specs,
        out_specs=pl.BlockSpec((qb, ATT_WIDTH), lambda b, i: (b * nq + i, 0)),
        out_shape=jax.ShapeDtypeStruct((m, ATT_WIDTH), BF16),
        compiler_params=_cparams(2),
        name="window_attn",
    )(q, k_ctx, v_ctx, *([k] * n_kblk), *([v] * n_kblk), sink)


def _dense_attn_kernel(*refs, n_kv_src, has_sink):
    q_ref = refs[0]
    k_refs = refs[1:1 + n_kv_src]
    v_refs = refs[1 + n_kv_src:1 + 2 * n_kv_src]
    rest = refs[1 + 2 * n_kv_src:]
    sink_ref = rest[0] if has_sink else None
    o_ref = rest[-1]
    if n_kv_src == 1:
        k, v = k_refs[0][...], v_refs[0][...]
    else:
        k = jnp.concatenate([r[...] for r in k_refs], axis=0)
        v = jnp.concatenate([r[...] for r in v_refs], axis=0)
    _attend(q_ref, slice(0, q_ref.shape[0]), k, v, None, sink_ref, o_ref)


def _dense_attention(q, ks, vs, sink, bsz, t_q):
    m = q.shape[0]
    qb = _pick_tile(t_q, (512, 256, 128))
    nq = t_q // qb
    kv_specs = [pl.BlockSpec((a.shape[0] // bsz, a.shape[1]), lambda b, i: (b, 0)) for a in ks]
    in_specs = [pl.BlockSpec((qb, ATT_WIDTH), lambda b, i: (b * nq + i, 0))] + kv_specs * 2
    args = [q, *ks, *vs]
    if sink is not None:
        in_specs.append(pl.BlockSpec(memory_space=pltpu.SMEM))
        args.append(sink)
    return pl.pallas_call(
        functools.partial(_dense_attn_kernel, n_kv_src=len(ks), has_sink=sink is not None),
        grid=(bsz, nq),
        in_specs=in_specs,
        out_specs=pl.BlockSpec((qb, ATT_WIDTH), lambda b, i: (b * nq + i, 0)),
        out_shape=jax.ShapeDtypeStruct((m, ATT_WIDTH), BF16),
        compiler_params=_cparams(2),
        name="dense_attn",
    )(*args)


def _split3(x):
    hi = x.astype(BF16)
    r1 = x - hi.astype(F32)
    mid = r1.astype(BF16)
    lo = (r1 - mid.astype(F32)).astype(BF16)
    return hi, mid, lo


GATE_QUANTITIES = 5
GQ_A, GQ_INTER, GQ_NRM, GQ_END, GQ_DECAY = range(GATE_QUANTITIES)


def _mlstm_gate_kernel(gt_ref, gbt_ref, m0_ref, o_ref, m_out, m_scr, *, rev):
    gi = pl.program_id(0)
    n_g = pl.num_programs(0)
    bsz, _, blk = gt_ref.shape
    rows = bsz * N_GATES
    n_chunks = blk // CHUNK

    @pl.when(gi == 0)
    def _():
        m_scr[...] = m0_ref[...]

    gates_t = (gt_ref[...] + gbt_ref[...]).reshape(rows, blk)
    r = lax.broadcasted_iota(jnp.int32, (blk, blk), 0)
    c = lax.broadcasted_iota(jnp.int32, (blk, blk), 1)
    feeds_b = jnp.where((r >= c) if rev else (r <= c), 1.0, 0.0).astype(BF16)
    brow = sum(_mm(piece, feeds_b) for piece in _split3(_log_sigmoid(gates_t)))
    b_all = pltpu.roll(brow, rows - C_HEADS, 0)
    rp = gates_t - b_all
    lane = lax.broadcasted_iota(jnp.int32, (rows, blk), 1)
    pos = lane % CHUNK
    seg = rp
    shift = 1
    while shift < CHUNK:
        if rev:
            seg = jnp.where(pos < CHUNK - shift, jnp.maximum(seg, pltpu.roll(seg, blk - shift, 1)), seg)
        else:
            seg = jnp.where(pos >= shift, jnp.maximum(seg, pltpu.roll(seg, shift, 1)), seg)
        shift *= 2
    m_init = m_scr[:, 0:1]
    m_prev = m_init
    b_off = jnp.zeros((rows, 1), F32)
    m_lane = jnp.zeros((rows, blk), F32)
    off_lane = jnp.zeros((rows, blk), F32)
    for ch in (range(n_chunks - 1, -1, -1) if rev else range(n_chunks)):
        end = ch * CHUNK if rev else (ch + 1) * CHUNK - 1
        here = (lane // CHUNK) == ch
        m_lane = jnp.where(here, m_prev, m_lane)
        off_lane = jnp.where(here, b_off, off_lane)
        b_end = b_all[:, end:end + 1]
        m_prev = (b_end - b_off) + jnp.maximum(m_prev, b_off + seg[:, end:end + 1])
        b_off = b_end
    m_last, b_last = m_prev, b_off
    a = off_lane - jnp.maximum(m_lane, off_lane + seg)
    quantities = [None] * GATE_QUANTITIES
    quantities[GQ_A] = a
    quantities[GQ_INTER] = jnp.exp(a + m_init)
    quantities[GQ_NRM] = jnp.exp(a - b_all)
    quantities[GQ_END] = jnp.exp(rp + (b_last - m_last))
    quantities[GQ_DECAY] = jnp.broadcast_to(jnp.exp(b_last + m_init - m_last), (rows, blk))
    o_ref[...] = jnp.concatenate([x.reshape(bsz, N_GATES, blk) for x in quantities], axis=1)
    m_scr[...] = jnp.broadcast_to(m_last, (rows, LANES))

    @pl.when(gi == n_g - 1)
    def _():
        m_out[...] = m_scr[...]


def _mlstm_gates(gt, gate_b, m_state, t_seq, rev):
    bsz = gt.shape[0]
    blk = _pick_tile(t_seq, (256, 128))
    ng = t_seq // blk
    tok = (lambda i: (0, 0, ng - 1 - i)) if rev else (lambda i: (0, 0, i))
    return pl.pallas_call(
        functools.partial(_mlstm_gate_kernel, rev=rev),
        grid=(ng,),
        in_specs=[pl.BlockSpec((bsz, N_GATES, blk), tok), _resident((1, N_GATES, 1)), _resident(m_state.shape)],
        out_specs=[pl.BlockSpec((bsz, GATE_QUANTITIES * N_GATES, blk), tok), _resident(m_state.shape)],
        out_shape=[jax.ShapeDtypeStruct((bsz, GATE_QUANTITIES * N_GATES, t_seq), F32),
                   jax.ShapeDtypeStruct(m_state.shape, F32)],
        scratch_shapes=[pltpu.VMEM(m_state.shape, F32)],
        compiler_params=_cparams(1),
        name="mlstm_gates_bwd" if rev else "mlstm_gates_fwd",
    )(gt, gate_b.reshape(1, N_GATES, 1), m_state)


N_DIR = 2


def _mlstm_kernel(*refs):
    per_dir = [refs[0:6], refs[6:12]]
    gb_ref = refs[12]
    h_refs, s_outs, s_scr = refs[13:15], refs[15:17], refs[17]
    gi = pl.program_id(1)
    n_g = pl.num_programs(1)
    blk = refs[0].shape[0]

    @pl.when(gi == 0)
    def _():
        for d in range(N_DIR):
            s_scr[d] = per_dir[d][5][0]

    r = lax.broadcasted_iota(jnp.int32, (blk, blk), 0)
    c = lax.broadcasted_iota(jnp.int32, (blk, blk), 1)
    ones = jnp.ones((C_HEAD_DIM, blk), BF16)
    units = []
    for d, rev in enumerate((False, True)):
        k_ref, qt_ref, vt_ref, g_ref, gq_ref, _ = per_dir[d]
        gates = g_ref[...] + gb_ref[...]
        feeds = (r >= c) if rev else (r <= c)
        feeds_tb = jnp.where((c >= r) if rev else (c <= r), 1.0, 0.0).astype(BF16)
        bcol = sum(_mm(feeds_tb, piece) for piece in _split3(_log_sigmoid(gates)))
        base = 2 * C_HEADS if rev else 0
        rp_col = gates[:, base:base + C_HEADS] - bcol[:, base + C_HEADS:base + 2 * C_HEADS]
        gq = gq_ref[0]
        for hd in range(C_HEADS):
            lanes = slice(hd * C_HEAD_DIM, (hd + 1) * C_HEAD_DIM)
            rows = [gq[q * N_GATES + base + hd:q * N_GATES + base + hd + 1] for q in range(GATE_QUANTITIES)]
            units.append(dict(
                d=d, hd=hd, lanes=lanes, feeds=feeds, rows=rows, rp=rp_col[:, hd:hd + 1],
                kh=k_ref[:, lanes],
                qth=qt_ref[lanes, :],
                v_aug=jnp.concatenate([vt_ref[lanes, :], ones], axis=0),
                st=s_scr[d, hd]))
    for u in units:
        u["s"] = _mm(u["kh"], u["qth"])
        u["inter"] = _mm(u["st"].astype(BF16), u["qth"])
        x_t = (u["rows"][GQ_END] * u["v_aug"].astype(F32)).astype(BF16)
        u["st_new"] = u["rows"][GQ_DECAY][:, :C_HEAD_DIM] * u["st"] + _mm(x_t, u["kh"])
    for u in units:
        w_t = jnp.exp(jnp.where(u["feeds"], u["rp"] + u["rows"][GQ_A], -jnp.inf))
        u["p"] = (u["s"] * w_t).astype(BF16)
    for u in units:
        tot = _mm(u["v_aug"], u["p"]) + u["rows"][GQ_INTER] * u["inter"]
        u["h_t"] = tot[:C_HEAD_DIM] / jnp.maximum(jnp.abs(tot[C_HEAD_DIM:]), u["rows"][GQ_NRM])
    for u in units:
        h_refs[u["d"]][:, u["lanes"]] = u["h_t"].T
        s_scr[u["d"], u["hd"]] = u["st_new"]

    @pl.when(gi == n_g - 1)
    def _():
        for d in range(N_DIR):
            s_outs[d][0] = s_scr[d]


def _mlstm_scan(k, qt, vt, g, gqs, gate_b, states, bsz, t_seq):
    m = k.shape[0]
    blk = _pick_tile(t_seq, (256, 128))
    ng = t_seq // blk
    st_spec = pl.BlockSpec((1, C_HEADS, 2 * C_HEAD_DIM, C_HEAD_DIM), lambda b, i: (b, 0, 0, 0))
    in_specs, args, h_specs = [], [], []
    for d, step in enumerate((lambda i: i, lambda i: ng - 1 - i)):
        tok = lambda b, i, step=step: (b * ng + step(i), 0)
        tok_t = lambda b, i, step=step: (0, b * ng + step(i))
        in_specs += [
            pl.BlockSpec((blk, C_WIDTH), tok),
            pl.BlockSpec((C_WIDTH, blk), tok_t),
            pl.BlockSpec((C_WIDTH, blk), tok_t),
            pl.BlockSpec((blk, N_GATES), tok),
            pl.BlockSpec((1, GATE_QUANTITIES * N_GATES, blk), lambda b, i, step=step: (b, 0, step(i))),
            st_spec,
        ]
        args += [k, qt, vt, g, gqs[d], states[d]]
        h_specs.append(pl.BlockSpec((blk, C_WIDTH), tok))
    return pl.pallas_call(
        _mlstm_kernel,
        grid=(bsz, ng),
        in_specs=in_specs + [_resident((1, N_GATES))],
        out_specs=h_specs + [st_spec] * N_DIR,
        out_shape=[jax.ShapeDtypeStruct((m, C_WIDTH), F32)] * N_DIR
                  + [jax.ShapeDtypeStruct(states[0].shape, F32)] * N_DIR,
        scratch_shapes=[pltpu.VMEM((N_DIR, C_HEADS, 2 * C_HEAD_DIM, C_HEAD_DIM), F32)],
        compiler_params=_cparams(2),
        name="mlstm_scan",
    )(*args, gate_b.reshape(1, N_GATES))


def _merge_kernel(x_ref, mod_ref, ng_ref, ya_ref, yb_ref, hf_ref, hb_ref, om_ref, hg_ref,
                  w_gl, w_br, w_out, o_ref):
    x = x_ref[...]
    mod = mod_ref[0]
    ng = ng_ref[...]
    h = _mod_norm(x, mod, ng, 1).astype(BF16)
    d = x.shape[1]
    gate_logits = lambda n: _mm(h, w_gl[:, n * d:(n + 1) * d])
    gl = gate_logits(0)
    za = _mm(ya_ref[...], w_br[0])
    hsum = hf_ref[...] + hb_ref[...]
    om = om_ref[...]
    ym = []
    for j in range(C_HEADS):
        lanes = slice(j * C_HEAD_DIM, (j + 1) * C_HEAD_DIM)
        ym.append((_sigmoid(om[:, lanes]) * _rms(hsum[:, lanes], hg_ref[:, lanes])).astype(BF16))
    gl_next = gate_logits(1)
    zb = _mm(yb_ref[...], w_br[1])
    acc = _sigmoid(gl) * za
    gl, gl_next = gl_next, gate_logits(2)
    zm = _mm(jnp.concatenate(ym, axis=1), w_br[2])
    acc = acc + _sigmoid(gl) * zb
    acc = acc + _sigmoid(gl_next) * zm
    y = _mm(acc.astype(BF16), w_out[...])
    o_ref[...] = _gated_residual(x, y, mod, ng, 1, 1.0)


def _merge(x, mod, ng, ya, yb, hf, hb, om, head_g, w_gl, w_br, w_out, rows_per_mod, mod_base):
    m, d = x.shape
    tm = _pick_tile(rows_per_mod)
    tiles_per_mod = rows_per_mod // tm
    row = lambda i: (i, 0)
    return pl.pallas_call(
        _merge_kernel,
        grid=(m // tm,),
        in_specs=[
            pl.BlockSpec((tm, d), row),
            _mod_spec(mod, lambda i: mod_base + i // tiles_per_mod),
            _resident_slab(ng),
            pl.BlockSpec((tm, ATT_WIDTH), row),
            pl.BlockSpec((tm, ATT_WIDTH), row),
            pl.BlockSpec((tm, C_WIDTH), row),
            pl.BlockSpec((tm, C_WIDTH), row),
            pl.BlockSpec((tm, C_WIDTH), row),
            _resident_slab(head_g),
            _resident_slab(w_gl),
            _resident_slab(w_br),
            _resident_slab(w_out),
        ],
        out_specs=pl.BlockSpec((tm, d), row),
        out_shape=jax.ShapeDtypeStruct((m, d), F32),
        compiler_params=_cparams(1),
        name="merge",
    )(x, mod.arr, ng.arr, ya, yb, hf, hb, om, head_g.arr, w_gl.arr, w_br.arr, w_out.arr)


def _rope_tables(t_seq):
    rows = t_seq // GRID_W
    row = jnp.repeat(jnp.arange(rows), GRID_W)
    col = jnp.tile(jnp.arange(GRID_W), rows)
    freqs = ROPE_THETA ** (-jnp.arange(ROPE_PAIRS, dtype=F32) / ROPE_PAIRS)
    ang = jnp.stack([row[:, None] * freqs, col[:, None] * freqs], axis=1)
    cos, sin = jnp.cos(ang), jnp.sin(ang)
    cos_head = jnp.concatenate([cos[:, 0], cos[:, 0], cos[:, 1], cos[:, 1]], axis=-1)
    sin_head = jnp.concatenate([-sin[:, 0], sin[:, 0], -sin[:, 1], sin[:, 1]], axis=-1)
    return jnp.tile(cos_head, (1, 2)), jnp.tile(sin_head, (1, 2))


def _split_in_weights(w_in):
    sizes = (ATT_WIDTH, N_KV * HEAD_DIM, N_KV * HEAD_DIM, ATT_WIDTH, N_KV * HEAD_DIM, N_KV * HEAD_DIM,
             C_WIDTH, C_WIDTH, C_WIDTH, C_WIDTH, N_GATES, N_BRANCH * w_in.shape[1])
    parts, start = [], 0
    for s in sizes:
        parts.append(w_in[..., start:start + s])
        start += s
    qa, ka, va, qb, kb, vb, qm, km, vm, om, gm, gl = parts
    cat = lambda ws: jnp.concatenate(ws, axis=-1)

    def doubled(w):
        return cat([w[..., :HEAD_DIM], w[..., :HEAD_DIM], w[..., HEAD_DIM:], w[..., HEAD_DIM:]])

    bf = lambda w: w.astype(BF16)
    weights = [bf(qa), bf(cat([doubled(ka), doubled(va)])), bf(qb), bf(cat([doubled(kb), doubled(vb)])),
               bf(cat([qm, km])), bf(jnp.swapaxes(cat([vm, gm]), -1, -2)), bf(om), bf(gm)]
    return weights, bf(gl)


def kernel(x, c, ctx, c_ctx, w_ada, b_ada, norm_g, ffn_w_gate, ffn_w_up, ffn_w_down, w_in, attn_sink,
           qk_norm_g, conv_w, conv_b, mlstm_gate_b, mlstm_norm_g, w_branch, w_out):
    bsz, t_lat, d = x.shape
    t_ctx = ctx.shape[1]
    depth = w_ada.shape[0]
    xs = x.reshape(bsz * t_lat, d)
    cs = ctx.reshape(bsz * t_ctx, d)
    mods = _modulation(jnp.concatenate([c, c_ctx[None]], axis=0), w_ada, b_ada)
    rope_tabs = _rope_tables(t_lat)
    zero_state = jnp.zeros((bsz, C_HEADS, 2 * C_HEAD_DIM, C_HEAD_DIM), F32)
    zero_m = jnp.zeros((bsz * N_GATES, LANES), F32)

    ffn_all = [w.astype(BF16) for w in (ffn_w_gate, ffn_w_up, ffn_w_down)]
    in_w_all, w_gl_all = _split_in_weights(w_in)
    w_br_all, w_o_all = w_branch.astype(BF16), w_out.astype(BF16)
    qkn_all = jnp.tile(qk_norm_g, (1, 1, 2))
    conv_b_all = conv_b[:, None, :]
    head_g_all = mlstm_norm_g[:, None, :]

    for l in range(depth):
        last = l == depth - 1
        mod, ng = _Slab(mods, l), _Slab(norm_g, l)
        ffn = [[_Slab(w, l, k) for w in ffn_all] for k in range(2)]
        in_w = [_Slab(w, l) for w in in_w_all]
        w_gl, w_br, w_o = _Slab(w_gl_all, l), _Slab(w_br_all, l), _Slab(w_o_all, l)
        qkn, cw, cb, head_g = _Slab(qkn_all, l), _Slab(conv_w, l), _Slab(conv_b_all, l), _Slab(head_g_all, l)
        sink = attn_sink[l].reshape(1, N_HEADS)
        lat = dict(rows_per_mod=t_lat, mod_base=0)
        con = dict(rows_per_mod=bsz * t_ctx, mod_base=bsz)

        xs = _half_ffn(xs, mod, ng, *ffn[0], 0, **lat)
        cs = _half_ffn(cs, mod, ng, *ffn[0], 0, **con)

        (qa, ka, va, qb, kb, vb, qmt, km, vm, om, gm, gmt) = _in_projection(
            xs, mod, ng, rope_tabs, qkn, in_w, cw, cb, t_seq=t_lat, **lat)
        (qa_c, ka_c, va_c, qb_c, kb_c, vb_c, qmt_c, km_c, vm_c, om_c, gm_c, gmt_c) = _in_projection(
            cs, mod, ng, None, qkn, in_w, cw, cb, t_seq=t_ctx, **con)

        ya = _window_attention(qa, ka, va, ka_c, va_c, sink, bsz, t_lat, t_ctx)
        yb = _dense_attention(qb, [kb_c, kb], [vb_c, vb], None, bsz, t_lat)

        gb = mlstm_gate_b[l]
        gq_cf, m_f = _mlstm_gates(gmt_c, gb, zero_m, t_ctx, False)
        gq_cb, m_b = _mlstm_gates(gmt_c, gb, zero_m, t_ctx, True)
        gq_f, _ = _mlstm_gates(gmt, gb, m_f, t_lat, False)
        gq_b, _ = _mlstm_gates(gmt, gb, m_b, t_lat, True)
        hf_c, hb_c, st_f, st_b = _mlstm_scan(km_c, qmt_c, vm_c, gm_c, (gq_cf, gq_cb), gb,
                                             (zero_state, zero_state), bsz, t_ctx)
        hf, hb, _, _ = _mlstm_scan(km, qmt, vm, gm, (gq_f, gq_b), gb, (st_f, st_b), bsz, t_lat)

        xs = _merge(xs, mod, ng, ya, yb, hf, hb, om, head_g, w_gl, w_br, w_o, **lat)
        xs = _half_ffn(xs, mod, ng, *ffn[1], 2, **lat)
        if not last:
            ya_c = _dense_attention(qa_c, [ka_c], [va_c], sink, bsz, t_ctx)
            yb_c = _dense_attention(qb_c, [kb_c], [vb_c], None, bsz, t_ctx)
            cs = _merge(cs, mod, ng, ya_c, yb_c, hf_c, hb_c, om_c, head_g, w_gl, w_br, w_o, **con)
            cs = _half_ffn(cs, mod, ng, *ffn[1], 2, **con)
    return xs.reshape(bsz, t_lat, d)
```

```python
import functools

import jax
import jax.numpy as jnp
from jax import lax
from jax.experimental import pallas as pl
from jax.experimental.pallas import tpu as pltpu

F32 = jnp.float32
BF16 = jnp.bfloat16

EPS = 1e-6
N_MOD = 9
GRID_W = 64
ROPE_THETA = 10000.0
HEAD_DIM = 64
ROPE_PAIRS = HEAD_DIM // 4
N_HEADS = 8
N_KV = 2
ATT_WIDTH = N_HEADS * HEAD_DIM
WINDOW = 128
C_HEADS = 4
C_HEAD_DIM = 128
C_WIDTH = C_HEADS * C_HEAD_DIM
C_CONV = 5
CHUNK = 64
N_GATES = 4 * C_HEADS
N_BRANCH = 3
LOG2E = 1.4426950408889634

LANES = 128
SUBLANES = 8
VMEM_LIMIT_BYTES = 56 * 1024 * 1024


def _cparams(n_axes):
    return pltpu.CompilerParams(
        dimension_semantics=("arbitrary",) * n_axes,
        vmem_limit_bytes=VMEM_LIMIT_BYTES,
    )


def _pick_tile(n, candidates=(512, 256, 128)):
    for c in candidates:
        if n % c == 0:
            return c
    raise ValueError(f"no tile size for {n}")


def _resident(shape):
    zeros = (0,) * len(shape)
    return pl.BlockSpec(shape, lambda *_: zeros, pipeline_mode=pl.Buffered(1))


class _Slab:
    def __init__(self, arr, *idx):
        self.arr, self.idx = arr, idx
        self.shape = arr.shape[len(idx):]


def _resident_slab(slab):
    index = slab.idx + (0,) * len(slab.shape)
    return pl.BlockSpec((None,) * len(slab.idx) + slab.shape, lambda *_: index, pipeline_mode=pl.Buffered(1))


def _mod_spec(mod, row_of_step):
    lead = mod.idx
    return pl.BlockSpec((None,) * len(lead) + (1,) + mod.shape[1:], lambda i: lead + (row_of_step(i), 0, 0))


def _mm(a, b):
    return jnp.dot(a, b, preferred_element_type=F32)


def _mm_nt(a, b):
    return lax.dot_general(a, b, (((1,), (1,)), ((), ())), preferred_element_type=F32)


def _rms(x, g):
    return x * lax.rsqrt(jnp.mean(x * x, axis=-1, keepdims=True) + EPS) * g


def _sigmoid(x):
    return 1.0 / (1.0 + jnp.exp(-x))


def _silu(x):
    return x * _sigmoid(x)


def _log_sigmoid(x):
    return jnp.minimum(x, 0.0) - jnp.log(1.0 + jnp.exp(-jnp.abs(x)))


def _mod_norm(x, mod, ng, k):
    return _rms(x, ng[2 * k:2 * k + 1]) * (1.0 + mod[3 * k + 1:3 * k + 2]) + mod[3 * k:3 * k + 1]


def _gated_residual(x, y, mod, ng, k, weight):
    return x + (weight * mod[3 * k + 2:3 * k + 3]) * _rms(y, ng[2 * k + 1:2 * k + 2])


def _mod_kernel(c_ref, w_ref, b_ref, o_ref):
    a = _silu(c_ref[...]).astype(BF16)
    o_ref[0] = _mm(a, w_ref[0].astype(BF16)) + b_ref[0]


def _modulation(c_all, w_ada, b_ada):
    n_layers, d, nd = w_ada.shape
    r = c_all.shape[0]
    out = pl.pallas_call(
        _mod_kernel,
        grid=(n_layers, nd // d),
        in_specs=[
            pl.BlockSpec((r, d), lambda l, j: (0, 0)),
            pl.BlockSpec((1, d, d), lambda l, j: (l, 0, j)),
            pl.BlockSpec((1, 1, d), lambda l, j: (l, 0, j)),
        ],
        out_specs=pl.BlockSpec((1, r, d), lambda l, j: (l, 0, j)),
        out_shape=jax.ShapeDtypeStruct((n_layers, r, nd), F32),
        compiler_params=_cparams(2),
        name="adaln_mod",
    )(c_all, w_ada, b_ada.reshape(n_layers, 1, nd))
    return out.reshape(n_layers, r, N_MOD, d)


FFN_COL_CHUNK = 256


def _ffn_kernel(x_ref, mod_ref, ng_ref, wg_ref, wu_ref, wd_ref, o_ref, a_scr, *, k):
    x = x_ref[...]
    mod = mod_ref[0]
    ng = ng_ref[...]
    h = _mod_norm(x, mod, ng, k).astype(BF16)
    n_chunks = wg_ref.shape[1] // FFN_COL_CHUNK
    cols = [slice(j * FFN_COL_CHUNK, (j + 1) * FFN_COL_CHUNK) for j in range(n_chunks)]
    gate_up = lambda j: (_mm(h, wg_ref[:, cols[j]]), _mm(h, wu_ref[:, cols[j]]))
    nxt = gate_up(0)
    for j in range(n_chunks):
        g, u = nxt
        if j + 1 < n_chunks:
            nxt = gate_up(j + 1)
        a_scr[:, cols[j]] = (_silu(g) * u).astype(BF16)
    y = _mm(a_scr[...], wd_ref[...])
    o_ref[...] = _gated_residual(x, y, mod, ng, k, 0.5)


def _half_ffn(x, mod, ng, wg, wu, wd, k, rows_per_mod, mod_base):
    m, d = x.shape
    d_ff = wg.shape[1]
    tm = _pick_tile(rows_per_mod, (1024, 512, 256, 128))
    tiles_per_mod = rows_per_mod // tm
    slabs = [ng, wg, wu, wd]
    return pl.pallas_call(
        functools.partial(_ffn_kernel, k=k),
        grid=(m // tm,),
        in_specs=[pl.BlockSpec((tm, d), lambda i: (i, 0)),
                  _mod_spec(mod, lambda i: mod_base + i // tiles_per_mod)] + [_resident_slab(s) for s in slabs],
        out_specs=pl.BlockSpec((tm, d), lambda i: (i, 0)),
        out_shape=jax.ShapeDtypeStruct((m, d), F32),
        scratch_shapes=[pltpu.VMEM((tm, d_ff), BF16)],
        compiler_params=_cparams(1),
        name=f"half_ffn{k}",
    )(x, mod.arr, *[s.arr for s in slabs])


def _head_rms(x, gain):
    low = lax.broadcasted_iota(jnp.int32, x.shape, 1) < HEAD_DIM
    ss = x * x
    s_lo = jnp.sum(jnp.where(low, ss, 0.0), axis=-1, keepdims=True)
    s_hi = jnp.sum(jnp.where(low, 0.0, ss), axis=-1, keepdims=True)
    inv = jnp.where(low, lax.rsqrt(s_lo * (1.0 / HEAD_DIM) + EPS), lax.rsqrt(s_hi * (1.0 / HEAD_DIM) + EPS))
    return x * inv * gain


def _rope(x, cos, sin_signed):
    lane = lax.broadcasted_iota(jnp.int32, x.shape, 1)
    first_half = (lane & ROPE_PAIRS) == 0
    partner = jnp.where(first_half, pltpu.roll(x, LANES - ROPE_PAIRS, 1), pltpu.roll(x, ROPE_PAIRS, 1))
    return x * cos + partner * sin_signed


CONV_HALO = SUBLANES


def _inproj_kernel(*refs, rope, tiles_per_seq):
    x_ref, prev_ref, next_ref, mod_ref, ng_ref = refs[:5]
    refs = refs[5:]
    if rope:
        cos, sin = refs[0][...], refs[1][...]
        refs = refs[2:]
    (qkn_ref, w_qa, w_kva, w_qb, w_kvb, w_qkm, w_vgt, w_om, cw_ref, cb_ref,
     qa_ref, ka_ref, va_ref, qb_ref, kb_ref, vb_ref, qmt_ref, km_ref, vm_ref, om_ref, gt_ref, xp) = refs
    tm = x_ref.shape[0]
    mod, ng = mod_ref[0], ng_ref[...]
    h = _mod_norm(x_ref[...], mod, ng, 1).astype(BF16)
    qn, kn = qkn_ref[0:1], qkn_ref[1:2]
    kv_w = 2 * N_KV * HEAD_DIM

    def finish(full, out_ref, width, norm_gain, scale):
        for j in range(width // LANES):
            blk = full[:, j * LANES:(j + 1) * LANES]
            if norm_gain is not None:
                blk = _head_rms(blk, norm_gain)
            if rope:
                blk = _rope(blk, cos, sin)
            if scale != 1.0:
                blk = blk * scale
            out_ref[:, j * LANES:(j + 1) * LANES] = blk.astype(BF16)

    q_scale = HEAD_DIM ** -0.5 * LOG2E
    halo = jnp.concatenate([prev_ref[...], next_ref[...]], axis=0)
    h_halo = _mod_norm(halo, mod, ng, 1).astype(BF16)
    qa = _mm(h, w_qa[...])
    kva = _mm(h, w_kva[...])
    finish(qa, qa_ref, ATT_WIDTH, None, q_scale)
    qb = _mm(h, w_qb[...])
    finish(kva, ka_ref, kv_w, None, 1.0)
    va_ref[...] = kva[:, kv_w:].astype(BF16)
    kvb = _mm(h, w_kvb[...])
    finish(qb, qb_ref, ATT_WIDTH, qn, q_scale)
    qk = _mm(jnp.concatenate([h, h_halo], axis=0), w_qkm[...])
    finish(kvb, kb_ref, kv_w, kn, 1.0)
    vb_ref[...] = kvb[:, kv_w:].astype(BF16)
    vgt = _mm_nt(w_vgt[...], h)
    vm_ref[...] = vgt[:C_WIDTH].astype(BF16)
    gt_ref[0] = vgt[C_WIDTH:]
    om_ref[...] = _mm(h, w_om[...])

    ti = pl.program_id(0) % tiles_per_seq
    zero = jnp.zeros((CONV_HALO, qk.shape[1]), F32)
    xp[0:CONV_HALO] = jnp.where(ti > 0, qk[tm:tm + CONV_HALO], zero)
    xp[CONV_HALO:CONV_HALO + tm] = qk[:tm]
    xp[CONV_HALO + tm:] = jnp.where(ti < tiles_per_seq - 1, qk[tm + CONV_HALO:], zero)
    first = CONV_HALO - C_CONV // 2
    acc = cb_ref[...] + cw_ref[0:1] * xp[first:first + tm]
    for j in range(1, C_CONV):
        acc = acc + cw_ref[j:j + 1] * xp[first + j:first + j + tm]
    y = _silu(acc)
    qmt_ref[...] = (y[:, :C_WIDTH] * (C_HEAD_DIM ** -0.5)).T.astype(BF16)
    km_ref[...] = y[:, C_WIDTH:].astype(BF16)


def _in_projection(x, mod, ng, rope_tabs, qkn, weights, conv_w, conv_b, rows_per_mod, mod_base, t_seq):
    m, d = x.shape
    tm = _pick_tile(min(rows_per_mod, t_seq), (1024, 512, 256, 128))
    tiles_per_mod = rows_per_mod // tm
    tiles_per_seq = t_seq // tm
    rope = rope_tabs is not None
    row = lambda i: (i, 0)
    hb = tm // CONV_HALO
    n_hb = m // CONV_HALO
    in_specs = [
        pl.BlockSpec((tm, d), row),
        pl.BlockSpec((CONV_HALO, d), lambda i: (jnp.maximum(i * hb - 1, 0), 0)),
        pl.BlockSpec((CONV_HALO, d), lambda i: (jnp.minimum((i + 1) * hb, n_hb - 1), 0)),
        _mod_spec(mod, lambda i: mod_base + i // tiles_per_mod),
        _resident_slab(ng),
    ]
    args = [x, x, x, mod.arr, ng.arr]
    if rope:
        in_specs += [pl.BlockSpec((tm, LANES), lambda i: (i % tiles_per_seq, 0))] * 2
        args += list(rope_tabs)
    slabs = [qkn] + list(weights) + [conv_w, conv_b]
    in_specs += [_resident_slab(s) for s in slabs]
    args += [s.arr for s in slabs]
    kv_w = 2 * N_KV * HEAD_DIM
    col = lambda i: (0, i)
    outs = [(ATT_WIDTH, BF16, False), (kv_w, BF16, False), (kv_w, BF16, False),
            (ATT_WIDTH, BF16, False), (kv_w, BF16, False), (kv_w, BF16, False),
            (C_WIDTH, BF16, True), (C_WIDTH, BF16, False), (C_WIDTH, BF16, True), (C_WIDTH, F32, False)]
    out_specs = [pl.BlockSpec((w, tm), col) if fm else pl.BlockSpec((tm, w), row) for w, _, fm in outs]
    out_shape = [jax.ShapeDtypeStruct((w, m) if fm else (m, w), dt) for w, dt, fm in outs]
    out_specs.append(pl.BlockSpec((1, N_GATES, tm), lambda i: (i // tiles_per_seq, 0, i % tiles_per_seq)))
    out_shape.append(jax.ShapeDtypeStruct((m // t_seq, N_GATES, t_seq), F32))
    return pl.pallas_call(
        functools.partial(_inproj_kernel, rope=rope, tiles_per_seq=tiles_per_seq),
        grid=(m // tm,),
        in_specs=in_specs,
        out_specs=out_specs,
        out_shape=out_shape,
        scratch_shapes=[pltpu.VMEM((tm + 2 * CONV_HALO, 2 * C_WIDTH), F32)],
        compiler_params=_cparams(1),
        name="in_proj_rope" if rope else "in_proj",
    )(*args)


DENSE_SMALL_KEYS = 1024
SCORE_LOOKAHEAD = 1


def _attend(q_ref, rows, k, v, bias, sink_ref, o_ref):
    nr = rows.stop - rows.start
    nk = k.shape[0]
    gsz = N_HEADS // N_KV
    hpp = 2 if nk <= DENSE_SMALL_KEYS else 1
    upper = lax.broadcasted_iota(jnp.int32, (nr, LANES), 1) >= HEAD_DIM

    def scores(first):
        grp = first // gsz
        qs = []
        for hd in range(first, first + hpp):
            qblk = q_ref[rows, (hd // 2) * LANES:(hd // 2 + 1) * LANES]
            mine = upper if hd % 2 else jnp.logical_not(upper)
            qs.append(jnp.where(mine, qblk, jnp.zeros_like(qblk)))
        q_stack = qs[0] if hpp == 1 else jnp.concatenate(qs, axis=0)
        s = _mm_nt(q_stack, k[:, grp * LANES:(grp + 1) * LANES]).reshape(hpp, nr, nk)
        return s if bias is None else s + bias[None]

    def softmax(first, s):
        mx = jnp.max(s, axis=-1, keepdims=True)
        if sink_ref is not None:
            hidx = lax.broadcasted_iota(jnp.int32, (hpp, 1, 1), 0)
            sink = jnp.zeros((hpp, 1, 1), F32)
            for i in range(hpp):
                sink = jnp.where(hidx == i, sink_ref[0, first + i] * LOG2E, sink)
            mx = jnp.maximum(mx, sink)
        p = jnp.exp2(s - mx)
        den = jnp.sum(p, axis=-1, keepdims=True)
        if sink_ref is not None:
            den = den + jnp.exp2(sink - mx)
        return p.astype(BF16).reshape(hpp * nr, nk), den

    def weighted_values(first, p, den):
        grp = first // gsz
        o = _mm(p, v[:, grp * LANES:(grp + 1) * LANES]).reshape(hpp, nr, LANES) / den
        return [o[i] for i in range(hpp)]

    passes = list(range(0, N_HEADS, hpp))
    n_pass = len(passes)
    outs = []
    s_buf = {i: scores(passes[i]) for i in range(min(SCORE_LOOKAHEAD, n_pass))}
    for i in range(n_pass):
        if i + SCORE_LOOKAHEAD < n_pass:
            s_buf[i + SCORE_LOOKAHEAD] = scores(passes[i + SCORE_LOOKAHEAD])
        outs += weighted_values(passes[i], *softmax(passes[i], s_buf.pop(i)))
    for j in range(N_HEADS // 2):
        o_ref[rows, j * LANES:(j + 1) * LANES] = jnp.where(upper, outs[2 * j + 1], outs[2 * j]).astype(BF16)


def _window_attn_kernel(*refs, n_kblk, t_seq):
    q_ref, kc_ref, vc_ref = refs[:3]
    kw = refs[3:3 + n_kblk]
    vw = refs[3 + n_kblk:3 + 2 * n_kblk]
    sink_ref, o_ref = refs[3 + 2 * n_kblk:]
    qb = q_ref.shape[0]
    t_ctx = kc_ref.shape[0]
    k = jnp.concatenate([kc_ref[...]] + [r[...] for r in kw], axis=0)
    v = jnp.concatenate([vc_ref[...]] + [r[...] for r in vw], axis=0)
    nk = k.shape[0]
    row = lax.broadcasted_iota(jnp.int32, (qb, nk), 0)
    col = lax.broadcasted_iota(jnp.int32, (qb, nk), 1)
    rel = col - t_ctx - WINDOW
    pos = rel + pl.program_id(1) * qb
    visible = (col < t_ctx) | ((jnp.abs(rel - row) <= WINDOW) & (pos >= 0) & (pos < t_seq))
    bias = jnp.where(visible, 0.0, -jnp.inf)
    _attend(q_ref, slice(0, qb), k, v, bias, sink_ref, o_ref)


def _window_attention(q, k, v, k_ctx, v_ctx, sink, bsz, t_seq, t_ctx):
    m = q.shape[0]
    qb = _pick_tile(t_seq, (256, 128))
    nq = t_seq // qb
    kblk_per_q = qb // WINDOW
    n_kblk = kblk_per_q + 2
    nkb_seq = t_seq // WINDOW
    kvw = k.shape[1]

    def win_spec(j):
        def idx(b, i):
            blk = jnp.clip(i * kblk_per_q + (j - 1), 0, nkb_seq - 1)
            return (b * nkb_seq + blk, 0)
        return pl.BlockSpec((WINDOW, kvw), idx)

    ctx_spec = pl.BlockSpec((t_ctx, kvw), lambda b, i: (b, 0))
    in_specs = ([pl.BlockSpec((qb, ATT_WIDTH), lambda b, i: (b * nq + i, 0)), ctx_spec, ctx_spec]
                + [win_spec(j) for j in range(n_kblk)] * 2
                + [pl.BlockSpec(memory_space=pltpu.SMEM)])
    return pl.pallas_call(
        functools.partial(_window_attn_kernel, n_kblk=n_kblk, t_seq=t_seq),
        grid=(bsz, nq),
        in_specs=in_specs,
        out_specs=pl.BlockSpec((qb, ATT_WIDTH), lambda b, i: (b * nq + i, 0)),
        out_shape=jax.ShapeDtypeStruct((m, ATT_WIDTH), BF16),
        compiler_params=_cparams(2),
        name="window_attn",
    )(q, k_ctx, v_ctx, *([k] * n_kblk), *([v] * n_kblk), sink)


def _dense_attn_kernel(*refs, n_kv_src, has_sink):
    q_ref = refs[0]
    k_refs = refs[1:1 + n_kv_src]
    v_refs = refs[1 + n_kv_src:1 + 2 * n_kv_src]
    rest = refs[1 + 2 * n_kv_src:]
    sink_ref = rest[0] if has_sink else None
    o_ref = rest[-1]
    if n_kv_src == 1:
        k, v = k_refs[0][...], v_refs[0][...]
    else:
        k = jnp.concatenate([r[...] for r in k_refs], axis=0)
        v = jnp.concatenate([r[...] for r in v_refs], axis=0)
    _attend(q_ref, slice(0, q_ref.shape[0]), k, v, None, sink_ref, o_ref)


def _dense_attention(q, ks, vs, sink, bsz, t_q):
    m = q.shape[0]
    qb = _pick_tile(t_q, (512, 256, 128))
    nq = t_q // qb
    kv_specs = [pl.BlockSpec((a.shape[0] // bsz, a.shape[1]), lambda b, i: (b, 0)) for a in ks]
    in_specs = [pl.BlockSpec((qb, ATT_WIDTH), lambda b, i: (b * nq + i, 0))] + kv_specs * 2
    args = [q, *ks, *vs]
    if sink is not None:
        in_specs.append(pl.BlockSpec(memory_space=pltpu.SMEM))
        args.append(sink)
    return pl.pallas_call(
        functools.partial(_dense_attn_kernel, n_kv_src=len(ks), has_sink=sink is not None),
        grid=(bsz, nq),
        in_specs=in_specs,
        out_specs=pl.BlockSpec((qb, ATT_WIDTH), lambda b, i: (b * nq + i, 0)),
        out_shape=jax.ShapeDtypeStruct((m, ATT_WIDTH), BF16),
        compiler_params=_cparams(2),
        name="dense_attn",
    )(*args)


def _split3(x):
    hi = x.astype(BF16)
    r1 = x - hi.astype(F32)
    mid = r1.astype(BF16)
    lo = (r1 - mid.astype(F32)).astype(BF16)
    return hi, mid, lo


GATE_QUANTITIES = 5
GQ_A, GQ_INTER, GQ_NRM, GQ_END, GQ_DECAY = range(GATE_QUANTITIES)


def _mlstm_gate_kernel(gt_ref, gbt_ref, m0_ref, o_ref, rpc_ref, m_out, m_scr, *, rev):
    gi = pl.program_id(0)
    n_g = pl.num_programs(0)
    bsz, _, blk = gt_ref.shape
    rows = bsz * N_GATES
    n_chunks = blk // CHUNK

    @pl.when(gi == 0)
    def _():
        m_scr[...] = m0_ref[...]

    gates_t = (gt_ref[...] + gbt_ref[...]).reshape(rows, blk)
    r = lax.broadcasted_iota(jnp.int32, (blk, blk), 0)
    c = lax.broadcasted_iota(jnp.int32, (blk, blk), 1)
    feeds_b = jnp.where((r >= c) if rev else (r <= c), 1.0, 0.0).astype(BF16)
    brow = sum(_mm(piece, feeds_b) for piece in _split3(_log_sigmoid(gates_t)))
    b_all = pltpu.roll(brow, rows - C_HEADS, 0)
    rp = gates_t - b_all
    lane = lax.broadcasted_iota(jnp.int32, (rows, blk), 1)
    pos = lane % CHUNK
    seg = rp
    shift = 1
    while shift < CHUNK:
        if rev:
            seg = jnp.where(pos < CHUNK - shift, jnp.maximum(seg, pltpu.roll(seg, blk - shift, 1)), seg)
        else:
            seg = jnp.where(pos >= shift, jnp.maximum(seg, pltpu.roll(seg, shift, 1)), seg)
        shift *= 2
    m_init = m_scr[:, 0:1]
    m_prev = m_init
    b_off = jnp.zeros((rows, 1), F32)
    m_lane = jnp.zeros((rows, blk), F32)
    off_lane = jnp.zeros((rows, blk), F32)
    for ch in (range(n_chunks - 1, -1, -1) if rev else range(n_chunks)):
        end = ch * CHUNK if rev else (ch + 1) * CHUNK - 1
        here = (lane // CHUNK) == ch
        m_lane = jnp.where(here, m_prev, m_lane)
        off_lane = jnp.where(here, b_off, off_lane)
        b_end = b_all[:, end:end + 1]
        m_prev = (b_end - b_off) + jnp.maximum(m_prev, b_off + seg[:, end:end + 1])
        b_off = b_end
    m_last, b_last = m_prev, b_off
    a = off_lane - jnp.maximum(m_lane, off_lane + seg)
    quantities = [None] * GATE_QUANTITIES
    quantities[GQ_A] = a
    quantities[GQ_INTER] = jnp.exp(a + m_init)
    quantities[GQ_NRM] = jnp.exp(a - b_all)
    quantities[GQ_END] = jnp.exp(rp + (b_last - m_last))
    quantities[GQ_DECAY] = jnp.broadcast_to(jnp.exp(b_last + m_init - m_last), (rows, blk))
    o_ref[...] = jnp.concatenate([x.reshape(bsz, N_GATES, blk) for x in quantities], axis=1)
    rp_t = rp.T
    for b in range(bsz):
        rpc_ref[b] = rp_t[:, b * N_GATES:(b + 1) * N_GATES]
    m_scr[...] = jnp.broadcast_to(m_last, (rows, LANES))

    @pl.when(gi == n_g - 1)
    def _():
        m_out[...] = m_scr[...]


def _mlstm_gates(gt, gate_b, m_state, t_seq, rev):
    bsz = gt.shape[0]
    blk = _pick_tile(t_seq, (256, 128))
    ng = t_seq // blk
    step = (lambda i: ng - 1 - i) if rev else (lambda i: i)
    tok = lambda i: (0, 0, step(i))
    return pl.pallas_call(
        functools.partial(_mlstm_gate_kernel, rev=rev),
        grid=(ng,),
        in_specs=[pl.BlockSpec((bsz, N_GATES, blk), tok), _resident((1, N_GATES, 1)), _resident(m_state.shape)],
        out_specs=[pl.BlockSpec((bsz, GATE_QUANTITIES * N_GATES, blk), tok),
                   pl.BlockSpec((bsz, blk, N_GATES), lambda i: (0, step(i), 0)),
                   _resident(m_state.shape)],
        out_shape=[jax.ShapeDtypeStruct((bsz, GATE_QUANTITIES * N_GATES, t_seq), F32),
                   jax.ShapeDtypeStruct((bsz, t_seq, N_GATES), F32),
                   jax.ShapeDtypeStruct(m_state.shape, F32)],
        scratch_shapes=[pltpu.VMEM(m_state.shape, F32)],
        compiler_params=_cparams(1),
        name="mlstm_gates_bwd" if rev else "mlstm_gates_fwd",
    )(gt, gate_b.reshape(1, N_GATES, 1), m_state)


N_DIR = 2


def _mlstm_kernel(*refs):
    per_dir = [refs[0:6], refs[6:12]]
    h_refs, s_outs, s_scr = refs[12:14], refs[14:16], refs[16]
    gi = pl.program_id(1)
    n_g = pl.num_programs(1)
    blk = refs[0].shape[0]

    @pl.when(gi == 0)
    def _():
        for d in range(N_DIR):
            s_scr[d] = per_dir[d][5][0]

    r = lax.broadcasted_iota(jnp.int32, (blk, blk), 0)
    c = lax.broadcasted_iota(jnp.int32, (blk, blk), 1)
    ones = jnp.ones((C_HEAD_DIM, blk), BF16)
    units = []
    for d, rev in enumerate((False, True)):
        k_ref, qt_ref, vt_ref, rpc_ref, gq_ref, _ = per_dir[d]
        feeds = (r >= c) if rev else (r <= c)
        base = 2 * C_HEADS if rev else 0
        rp_col = rpc_ref[0][:, base:base + C_HEADS]
        gq = gq_ref[0]
        for hd in range(C_HEADS):
            lanes = slice(hd * C_HEAD_DIM, (hd + 1) * C_HEAD_DIM)
            rows = [gq[q * N_GATES + base + hd:q * N_GATES + base + hd + 1] for q in range(GATE_QUANTITIES)]
            units.append(dict(
                d=d, hd=hd, lanes=lanes, feeds=feeds, rows=rows, rp=rp_col[:, hd:hd + 1],
                kh=k_ref[:, lanes],
                qth=qt_ref[lanes, :],
                v_aug=jnp.concatenate([vt_ref[lanes, :], ones], axis=0),
                st=s_scr[d, hd]))
    for u in units:
        u["s"] = _mm(u["kh"], u["qth"])
        u["inter"] = _mm(u["st"].astype(BF16), u["qth"])
        x_t = (u["rows"][GQ_END] * u["v_aug"].astype(F32)).astype(BF16)
        u["st_new"] = u["rows"][GQ_DECAY][:, :C_HEAD_DIM] * u["st"] + _mm(x_t, u["kh"])
    for u in units:
        w_t = jnp.exp(jnp.where(u["feeds"], u["rp"] + u["rows"][GQ_A], -jnp.inf))
        u["p"] = (u["s"] * w_t).astype(BF16)
    for u in units:
        tot = _mm(u["v_aug"], u["p"]) + u["rows"][GQ_INTER] * u["inter"]
        u["h_t"] = tot[:C_HEAD_DIM] / jnp.maximum(jnp.abs(tot[C_HEAD_DIM:]), u["rows"][GQ_NRM])
    for u in units:
        h_refs[u["d"]][:, u["lanes"]] = u["h_t"].T
        s_scr[u["d"], u["hd"]] = u["st_new"]

    @pl.when(gi == n_g - 1)
    def _():
        for d in range(N_DIR):
            s_outs[d][0] = s_scr[d]


def _mlstm_scan(k, qt, vt, rpcs, gqs, states, bsz, t_seq):
    m = k.shape[0]
    blk = _pick_tile(t_seq, (256, 128))
    ng = t_seq // blk
    st_spec = pl.BlockSpec((1, C_HEADS, 2 * C_HEAD_DIM, C_HEAD_DIM), lambda b, i: (b, 0, 0, 0))
    in_specs, args, h_specs = [], [], []
    for d, step in enumerate((lambda i: i, lambda i: ng - 1 - i)):
        tok = lambda b, i, step=step: (b * ng + step(i), 0)
        tok_t = lambda b, i, step=step: (0, b * ng + step(i))
        in_specs += [
            pl.BlockSpec((blk, C_WIDTH), tok),
            pl.BlockSpec((C_WIDTH, blk), tok_t),
            pl.BlockSpec((C_WIDTH, blk), tok_t),
            pl.BlockSpec((1, blk, N_GATES), lambda b, i, step=step: (b, step(i), 0)),
            pl.BlockSpec((1, GATE_QUANTITIES * N_GATES, blk), lambda b, i, step=step: (b, 0, step(i))),
            st_spec,
        ]
        args += [k, qt, vt, rpcs[d], gqs[d], states[d]]
        h_specs.append(pl.BlockSpec((blk, C_WIDTH), tok))
    return pl.pallas_call(
        _mlstm_kernel,
        grid=(bsz, ng),
        in_specs=in_specs,
        out_specs=h_specs + [st_spec] * N_DIR,
        out_shape=[jax.ShapeDtypeStruct((m, C_WIDTH), F32)] * N_DIR
                  + [jax.ShapeDtypeStruct(states[0].shape, F32)] * N_DIR,
        scratch_shapes=[pltpu.VMEM((N_DIR, C_HEADS, 2 * C_HEAD_DIM, C_HEAD_DIM), F32)],
        compiler_params=_cparams(2),
        name="mlstm_scan",
    )(*args)


def _merge_kernel(x_ref, mod_ref, ng_ref, ya_ref, yb_ref, hf_ref, hb_ref, om_ref, hg_ref,
                  w_gl, w_br, w_out, o_ref):
    x = x_ref[...]
    mod = mod_ref[0]
    ng = ng_ref[...]
    h = _mod_norm(x, mod, ng, 1).astype(BF16)
    d = x.shape[1]
    gate_logits = lambda n: _mm(h, w_gl[:, n * d:(n + 1) * d])
    gl = gate_logits(0)
    za = _mm(ya_ref[...], w_br[0])
    hsum = hf_ref[...] + hb_ref[...]
    om = om_ref[...]
    ym = []
    for j in range(C_HEADS):
        lanes = slice(j * C_HEAD_DIM, (j + 1) * C_HEAD_DIM)
        ym.append((_sigmoid(om[:, lanes]) * _rms(hsum[:, lanes], hg_ref[:, lanes])).astype(BF16))
    gl_next = gate_logits(1)
    zb = _mm(yb_ref[...], w_br[1])
    acc = _sigmoid(gl) * za
    gl, gl_next = gl_next, gate_logits(2)
    zm = _mm(jnp.concatenate(ym, axis=1), w_br[2])
    acc = acc + _sigmoid(gl) * zb
    acc = acc + _sigmoid(gl_next) * zm
    y = _mm(acc.astype(BF16), w_out[...])
    o_ref[...] = _gated_residual(x, y, mod, ng, 1, 1.0)


def _merge(x, mod, ng, ya, yb, hf, hb, om, head_g, w_gl, w_br, w_out, rows_per_mod, mod_base):
    m, d = x.shape
    tm = _pick_tile(rows_per_mod)
    tiles_per_mod = rows_per_mod // tm
    row = lambda i: (i, 0)
    return pl.pallas_call(
        _merge_kernel,
        grid=(m // tm,),
        in_specs=[
            pl.BlockSpec((tm, d), row),
            _mod_spec(mod, lambda i: mod_base + i // tiles_per_mod),
            _resident_slab(ng),
            pl.BlockSpec((tm, ATT_WIDTH), row),
            pl.BlockSpec((tm, ATT_WIDTH), row),
            pl.BlockSpec((tm, C_WIDTH), row),
            pl.BlockSpec((tm, C_WIDTH), row),
            pl.BlockSpec((tm, C_WIDTH), row),
            _resident_slab(head_g),
            _resident_slab(w_gl),
            _resident_slab(w_br),
            _resident_slab(w_out),
        ],
        out_specs=pl.BlockSpec((tm, d), row),
        out_shape=jax.ShapeDtypeStruct((m, d), F32),
        compiler_params=_cparams(1),
        name="merge",
    )(x, mod.arr, ng.arr, ya, yb, hf, hb, om, head_g.arr, w_gl.arr, w_br.arr, w_out.arr)


def _rope_tables(t_seq):
    rows = t_seq // GRID_W
    row = jnp.repeat(jnp.arange(rows), GRID_W)
    col = jnp.tile(jnp.arange(GRID_W), rows)
    freqs = ROPE_THETA ** (-jnp.arange(ROPE_PAIRS, dtype=F32) / ROPE_PAIRS)
    ang = jnp.stack([row[:, None] * freqs, col[:, None] * freqs], axis=1)
    cos, sin = jnp.cos(ang), jnp.sin(ang)
    cos_head = jnp.concatenate([cos[:, 0], cos[:, 0], cos[:, 1], cos[:, 1]], axis=-1)
    sin_head = jnp.concatenate([-sin[:, 0], sin[:, 0], -sin[:, 1], sin[:, 1]], axis=-1)
    return jnp.tile(cos_head, (1, 2)), jnp.tile(sin_head, (1, 2))


def _split_in_weights(w_in):
    sizes = (ATT_WIDTH, N_KV * HEAD_DIM, N_KV * HEAD_DIM, ATT_WIDTH, N_KV * HEAD_DIM, N_KV * HEAD_DIM,
             C_WIDTH, C_WIDTH, C_WIDTH, C_WIDTH, N_GATES, N_BRANCH * w_in.shape[1])
    parts, start = [], 0
    for s in sizes:
        parts.append(w_in[..., start:start + s])
        start += s
    qa, ka, va, qb, kb, vb, qm, km, vm, om, gm, gl = parts
    cat = lambda ws: jnp.concatenate(ws, axis=-1)

    def doubled(w):
        return cat([w[..., :HEAD_DIM], w[..., :HEAD_DIM], w[..., HEAD_DIM:], w[..., HEAD_DIM:]])

    bf = lambda w: w.astype(BF16)
    weights = [bf(qa), bf(cat([doubled(ka), doubled(va)])), bf(qb), bf(cat([doubled(kb), doubled(vb)])),
               bf(cat([qm, km])), bf(jnp.swapaxes(cat([vm, gm]), -1, -2)), bf(om)]
    return weights, bf(gl)


def kernel(x, c, ctx, c_ctx, w_ada, b_ada, norm_g, ffn_w_gate, ffn_w_up, ffn_w_down, w_in, attn_sink,
           qk_norm_g, conv_w, conv_b, mlstm_gate_b, mlstm_norm_g, w_branch, w_out):
    bsz, t_lat, d = x.shape
    t_ctx = ctx.shape[1]
    depth = w_ada.shape[0]
    xs = x.reshape(bsz * t_lat, d)
    cs = ctx.reshape(bsz * t_ctx, d)
    mods = _modulation(jnp.concatenate([c, c_ctx[None]], axis=0), w_ada, b_ada)
    rope_tabs = _rope_tables(t_lat)
    zero_state = jnp.zeros((bsz, C_HEADS, 2 * C_HEAD_DIM, C_HEAD_DIM), F32)
    zero_m = jnp.zeros((bsz * N_GATES, LANES), F32)

    ffn_all = [w.astype(BF16) for w in (ffn_w_gate, ffn_w_up, ffn_w_down)]
    in_w_all, w_gl_all = _split_in_weights(w_in)
    w_br_all, w_o_all = w_branch.astype(BF16), w_out.astype(BF16)
    qkn_all = jnp.tile(qk_norm_g, (1, 1, 2))
    conv_b_all = conv_b[:, None, :]
    head_g_all = mlstm_norm_g[:, None, :]

    for l in range(depth):
        last = l == depth - 1
        mod, ng = _Slab(mods, l), _Slab(norm_g, l)
        ffn = [[_Slab(w, l, k) for w in ffn_all] for k in range(2)]
        in_w = [_Slab(w, l) for w in in_w_all]
        w_gl, w_br, w_o = _Slab(w_gl_all, l), _Slab(w_br_all, l), _Slab(w_o_all, l)
        qkn, cw, cb, head_g = _Slab(qkn_all, l), _Slab(conv_w, l), _Slab(conv_b_all, l), _Slab(head_g_all, l)
        sink = attn_sink[l].reshape(1, N_HEADS)
        lat = dict(rows_per_mod=t_lat, mod_base=0)
        con = dict(rows_per_mod=bsz * t_ctx, mod_base=bsz)

        xs = _half_ffn(xs, mod, ng, *ffn[0], 0, **lat)
        cs = _half_ffn(cs, mod, ng, *ffn[0], 0, **con)

        (qa, ka, va, qb, kb, vb, qmt, km, vm, om, gmt) = _in_projection(
            xs, mod, ng, rope_tabs, qkn, in_w, cw, cb, t_seq=t_lat, **lat)
        (qa_c, ka_c, va_c, qb_c, kb_c, vb_c, qmt_c, km_c, vm_c, om_c, gmt_c) = _in_projection(
            cs, mod, ng, None, qkn, in_w, cw, cb, t_seq=t_ctx, **con)

        ya = _window_attention(qa, ka, va, ka_c, va_c, sink, bsz, t_lat, t_ctx)
        yb = _dense_attention(qb, [kb_c, kb], [vb_c, vb], None, bsz, t_lat)

        gb = mlstm_gate_b[l]
        gq_cf, rp_cf, m_f = _mlstm_gates(gmt_c, gb, zero_m, t_ctx, False)
        gq_cb, rp_cb, m_b = _mlstm_gates(gmt_c, gb, zero_m, t_ctx, True)
        gq_f, rp_f, _ = _mlstm_gates(gmt, gb, m_f, t_lat, False)
        gq_b, rp_b, _ = _mlstm_gates(gmt, gb, m_b, t_lat, True)
        hf_c, hb_c, st_f, st_b = _mlstm_scan(km_c, qmt_c, vm_c, (rp_cf, rp_cb), (gq_cf, gq_cb),
                                             (zero_state, zero_state), bsz, t_ctx)
        hf, hb, _, _ = _mlstm_scan(km, qmt, vm, (rp_f, rp_b), (gq_f, gq_b), (st_f, st_b), bsz, t_lat)

        xs = _merge(xs, mod, ng, ya, yb, hf, hb, om, head_g, w_gl, w_br, w_o, **lat)
        xs = _half_ffn(xs, mod, ng, *ffn[1], 2, **lat)
        if not last:
            ya_c = _dense_attention(qa_c, [ka_c], [va_c], sink, bsz, t_ctx)
            yb_c = _dense_attention(qb_c, [kb_c], [vb_c], None, bsz, t_ctx)
            cs = _merge(cs, mod, ng, ya_c, yb_c, hf_c, hb_c, om_c, head_g, w_gl, w_br, w_o, **con)
            cs = _half_ffn(cs, mod, ng, *ffn[1], 2, **con)
    return xs.reshape(bsz, t_lat, d)
```

```python
import functools

import jax
import jax.numpy as jnp
from jax import lax
from jax.experimental import pallas as pl
from jax.experimental.pallas import tpu as pltpu

F32 = jnp.float32
BF16 = jnp.bfloat16

EPS = 1e-6
N_MOD = 9
GRID_W = 64
ROPE_THETA = 10000.0
HEAD_DIM = 64
ROPE_PAIRS = HEAD_DIM // 4
N_HEADS = 8
N_KV = 2
ATT_WIDTH = N_HEADS * HEAD_DIM
WINDOW = 128
C_HEADS = 4
C_HEAD_DIM = 128
C_WIDTH = C_HEADS * C_HEAD_DIM
C_CONV = 5
CHUNK = 64
N_GATES = 4 * C_HEADS
N_BRANCH = 3
LOG2E = 1.4426950408889634

LANES = 128
SUBLANES = 8
VMEM_LIMIT_BYTES = 56 * 1024 * 1024


def _cparams(n_axes):
    return pltpu.CompilerParams(
        dimension_semantics=("arbitrary",) * n_axes,
        vmem_limit_bytes=VMEM_LIMIT_BYTES,
    )


def _pick_tile(n, candidates=(512, 256, 128)):
    for c in candidates:
        if n % c == 0:
            return c
    raise ValueError(f"no tile size for {n}")


def _resident(shape):
    zeros = (0,) * len(shape)
    return pl.BlockSpec(shape, lambda *_: zeros, pipeline_mode=pl.Buffered(1))


class _Slab:
    def __init__(self, arr, *idx):
        self.arr, self.idx = arr, idx
        self.shape = arr.shape[len(idx):]


def _resident_slab(slab):
    index = slab.idx + (0,) * len(slab.shape)
    return pl.BlockSpec((None,) * len(slab.idx) + slab.shape, lambda *_: index, pipeline_mode=pl.Buffered(1))


def _mod_spec(mod, row_of_step):
    lead = mod.idx
    return pl.BlockSpec((None,) * len(lead) + (1,) + mod.shape[1:], lambda i: lead + (row_of_step(i), 0, 0))


def _mm(a, b):
    return jnp.dot(a, b, preferred_element_type=F32)


def _mm_nt(a, b):
    return lax.dot_general(a, b, (((1,), (1,)), ((), ())), preferred_element_type=F32)


def _rms(x, g):
    return x * lax.rsqrt(jnp.mean(x * x, axis=-1, keepdims=True) + EPS) * g


def _sigmoid(x):
    return 1.0 / (1.0 + jnp.exp(-x))


def _silu(x):
    return x * _sigmoid(x)


def _log_sigmoid(x):
    return jnp.minimum(x, 0.0) - jnp.log(1.0 + jnp.exp(-jnp.abs(x)))


def _mod_norm(x, mod, ng, k):
    return _rms(x, ng[2 * k:2 * k + 1]) * (1.0 + mod[3 * k + 1:3 * k + 2]) + mod[3 * k:3 * k + 1]


def _gated_residual(x, y, mod, ng, k, weight):
    return x + (weight * mod[3 * k + 2:3 * k + 3]) * _rms(y, ng[2 * k + 1:2 * k + 2])


def _mod_kernel(c_ref, w_ref, b_ref, o_ref):
    a = _silu(c_ref[...]).astype(BF16)
    o_ref[0] = _mm(a, w_ref[0].astype(BF16)) + b_ref[0]


def _modulation(c_all, w_ada, b_ada):
    n_layers, d, nd = w_ada.shape
    r = c_all.shape[0]
    out = pl.pallas_call(
        _mod_kernel,
        grid=(n_layers, nd // d),
        in_specs=[
            pl.BlockSpec((r, d), lambda l, j: (0, 0)),
            pl.BlockSpec((1, d, d), lambda l, j: (l, 0, j)),
            pl.BlockSpec((1, 1, d), lambda l, j: (l, 0, j)),
        ],
        out_specs=pl.BlockSpec((1, r, d), lambda l, j: (l, 0, j)),
        out_shape=jax.ShapeDtypeStruct((n_layers, r, nd), F32),
        compiler_params=_cparams(2),
        name="adaln_mod",
    )(c_all, w_ada, b_ada.reshape(n_layers, 1, nd))
    return out.reshape(n_layers, r, N_MOD, d)


FFN_COL_CHUNK = 256


def _ffn_kernel(x_ref, mod_ref, ng_ref, wg_ref, wu_ref, wd_ref, o_ref, a_scr, *, k):
    x = x_ref[...]
    mod = mod_ref[0]
    ng = ng_ref[...]
    h = _mod_norm(x, mod, ng, k).astype(BF16)
    n_chunks = wg_ref.shape[1] // FFN_COL_CHUNK
    cols = [slice(j * FFN_COL_CHUNK, (j + 1) * FFN_COL_CHUNK) for j in range(n_chunks)]
    gate_up = lambda j: (_mm(h, wg_ref[:, cols[j]]), _mm(h, wu_ref[:, cols[j]]))
    nxt = gate_up(0)
    for j in range(n_chunks):
        g, u = nxt
        if j + 1 < n_chunks:
            nxt = gate_up(j + 1)
        a_scr[:, cols[j]] = (_silu(g) * u).astype(BF16)
    y = _mm(a_scr[...], wd_ref[...])
    o_ref[...] = _gated_residual(x, y, mod, ng, k, 0.5)


def _half_ffn(x, mod, ng, wg, wu, wd, k, rows_per_mod, mod_base):
    m, d = x.shape
    d_ff = wg.shape[1]
    tm = _pick_tile(rows_per_mod, (1024, 512, 256, 128))
    tiles_per_mod = rows_per_mod // tm
    slabs = [ng, wg, wu, wd]
    return pl.pallas_call(
        functools.partial(_ffn_kernel, k=k),
        grid=(m // tm,),
        in_specs=[pl.BlockSpec((tm, d), lambda i: (i, 0)),
                  _mod_spec(mod, lambda i: mod_base + i // tiles_per_mod)] + [_resident_slab(s) for s in slabs],
        out_specs=pl.BlockSpec((tm, d), lambda i: (i, 0)),
        out_shape=jax.ShapeDtypeStruct((m, d), F32),
        scratch_shapes=[pltpu.VMEM((tm, d_ff), BF16)],
        compiler_params=_cparams(1),
        name=f"half_ffn{k}",
    )(x, mod.arr, *[s.arr for s in slabs])


def _head_rms(x, gain):
    low = lax.broadcasted_iota(jnp.int32, x.shape, 1) < HEAD_DIM
    ss = x * x
    s_lo = jnp.sum(jnp.where(low, ss, 0.0), axis=-1, keepdims=True)
    s_hi = jnp.sum(jnp.where(low, 0.0, ss), axis=-1, keepdims=True)
    inv = jnp.where(low, lax.rsqrt(s_lo * (1.0 / HEAD_DIM) + EPS), lax.rsqrt(s_hi * (1.0 / HEAD_DIM) + EPS))
    return x * inv * gain


def _rope(x, cos, sin_signed):
    lane = lax.broadcasted_iota(jnp.int32, x.shape, 1)
    first_half = (lane & ROPE_PAIRS) == 0
    partner = jnp.where(first_half, pltpu.roll(x, LANES - ROPE_PAIRS, 1), pltpu.roll(x, ROPE_PAIRS, 1))
    return x * cos + partner * sin_signed


CONV_HALO = SUBLANES


def _inproj_kernel(*refs, rope, tiles_per_seq):
    x_ref, prev_ref, next_ref, mod_ref, ng_ref = refs[:5]
    refs = refs[5:]
    if rope:
        cos, sin = refs[0][...], refs[1][...]
        refs = refs[2:]
    (qkn_ref, w_qa, w_kva, w_qb, w_kvb, w_qkm, w_vgt, w_om, cw_ref, cb_ref,
     qa_ref, ka_ref, va_ref, qb_ref, kb_ref, vb_ref, qmt_ref, km_ref, vm_ref, om_ref, gt_ref, xp) = refs
    tm = x_ref.shape[0]
    mod, ng = mod_ref[0], ng_ref[...]
    h = _mod_norm(x_ref[...], mod, ng, 1).astype(BF16)
    qn, kn = qkn_ref[0:1], qkn_ref[1:2]
    kv_w = 2 * N_KV * HEAD_DIM

    def finish(full, out_ref, width, norm_gain, scale):
        for j in range(width // LANES):
            blk = full[:, j * LANES:(j + 1) * LANES]
            if norm_gain is not None:
                blk = _head_rms(blk, norm_gain)
            if rope:
                blk = _rope(blk, cos, sin)
            if scale != 1.0:
                blk = blk * scale
            out_ref[:, j * LANES:(j + 1) * LANES] = blk.astype(BF16)

    q_scale = HEAD_DIM ** -0.5 * LOG2E
    halo = jnp.concatenate([prev_ref[...], next_ref[...]], axis=0)
    h_halo = _mod_norm(halo, mod, ng, 1).astype(BF16)
    qa = _mm(h, w_qa[...])
    kva = _mm(h, w_kva[...])
    finish(qa, qa_ref, ATT_WIDTH, None, q_scale)
    qb = _mm(h, w_qb[...])
    finish(kva, ka_ref, kv_w, None, 1.0)
    va_ref[...] = kva[:, kv_w:].astype(BF16)
    kvb = _mm(h, w_kvb[...])
    finish(qb, qb_ref, ATT_WIDTH, qn, q_scale)
    qk = _mm(jnp.concatenate([h, h_halo], axis=0), w_qkm[...])
    finish(kvb, kb_ref, kv_w, kn, 1.0)
    vb_ref[...] = kvb[:, kv_w:].astype(BF16)
    vgt = _mm_nt(w_vgt[...], h)
    vm_ref[...] = vgt[:C_WIDTH].astype(BF16)
    gt_ref[0] = vgt[C_WIDTH:]
    om_ref[...] = _mm(h, w_om[...])

    ti = pl.program_id(0) % tiles_per_seq
    zero = jnp.zeros((CONV_HALO, qk.shape[1]), F32)
    xp[0:CONV_HALO] = jnp.where(ti > 0, qk[tm:tm + CONV_HALO], zero)
    xp[CONV_HALO:CONV_HALO + tm] = qk[:tm]
    xp[CONV_HALO + tm:] = jnp.where(ti < tiles_per_seq - 1, qk[tm + CONV_HALO:], zero)
    first = CONV_HALO - C_CONV // 2
    acc = cb_ref[...] + cw_ref[0:1] * xp[first:first + tm]
    for j in range(1, C_CONV):
        acc = acc + cw_ref[j:j + 1] * xp[first + j:first + j + tm]
    y = _silu(acc)
    qmt_ref[...] = (y[:, :C_WIDTH] * (C_HEAD_DIM ** -0.5)).T.astype(BF16)
    km_ref[...] = y[:, C_WIDTH:].astype(BF16)


def _in_projection(x, mod, ng, rope_tabs, qkn, weights, conv_w, conv_b, rows_per_mod, mod_base, t_seq):
    m, d = x.shape
    tm = _pick_tile(min(rows_per_mod, t_seq), (1024, 512, 256, 128))
    tiles_per_mod = rows_per_mod // tm
    tiles_per_seq = t_seq // tm
    rope = rope_tabs is not None
    row = lambda i: (i, 0)
    hb = tm // CONV_HALO
    n_hb = m // CONV_HALO
    in_specs = [
        pl.BlockSpec((tm, d), row),
        pl.BlockSpec((CONV_HALO, d), lambda i: (jnp.maximum(i * hb - 1, 0), 0)),
        pl.BlockSpec((CONV_HALO, d), lambda i: (jnp.minimum((i + 1) * hb, n_hb - 1), 0)),
        _mod_spec(mod, lambda i: mod_base + i // tiles_per_mod),
        _resident_slab(ng),
    ]
    args = [x, x, x, mod.arr, ng.arr]
    if rope:
        in_specs += [pl.BlockSpec((tm, LANES), lambda i: (i % tiles_per_seq, 0))] * 2
        args += list(rope_tabs)
    slabs = [qkn] + list(weights) + [conv_w, conv_b]
    in_specs += [_resident_slab(s) for s in slabs]
    args += [s.arr for s in slabs]
    kv_w = 2 * N_KV * HEAD_DIM
    col = lambda i: (0, i)
    outs = [(ATT_WIDTH, BF16, False), (kv_w, BF16, False), (kv_w, BF16, False),
            (ATT_WIDTH, BF16, False), (kv_w, BF16, False), (kv_w, BF16, False),
            (C_WIDTH, BF16, True), (C_WIDTH, BF16, False), (C_WIDTH, BF16, True), (C_WIDTH, F32, False)]
    out_specs = [pl.BlockSpec((w, tm), col) if fm else pl.BlockSpec((tm, w), row) for w, _, fm in outs]
    out_shape = [jax.ShapeDtypeStruct((w, m) if fm else (m, w), dt) for w, dt, fm in outs]
    out_specs.append(pl.BlockSpec((1, N_GATES, tm), lambda i: (i // tiles_per_seq, 0, i % tiles_per_seq)))
    out_shape.append(jax.ShapeDtypeStruct((m // t_seq, N_GATES, t_seq), F32))
    return pl.pallas_call(
        functools.partial(_inproj_kernel, rope=rope, tiles_per_seq=tiles_per_seq),
        grid=(m // tm,),
        in_specs=in_specs,
        out_specs=out_specs,
        out_shape=out_shape,
        scratch_shapes=[pltpu.VMEM((tm + 2 * CONV_HALO, 2 * C_WIDTH), F32)],
        compiler_params=_cparams(1),
        name="in_proj_rope" if rope else "in_proj",
    )(*args)


DENSE_SMALL_KEYS = 1024
SCORE_LOOKAHEAD = 1


def _attend(q_ref, rows, k, v, bias, sink_ref, o_ref):
    nr = rows.stop - rows.start
    nk = k.shape[0]
    gsz = N_HEADS // N_KV
    hpp = 2 if nk <= DENSE_SMALL_KEYS else 1
    upper = lax.broadcasted_iota(jnp.int32, (nr, LANES), 1) >= HEAD_DIM

    def scores(first):
        grp = first // gsz
        qs = []
        for hd in range(first, first + hpp):
            qblk = q_ref[rows, (hd // 2) * LANES:(hd // 2 + 1) * LANES]
            mine = upper if hd % 2 else jnp.logical_not(upper)
            qs.append(jnp.where(mine, qblk, jnp.zeros_like(qblk)))
        q_stack = qs[0] if hpp == 1 else jnp.concatenate(qs, axis=0)
        s = _mm_nt(q_stack, k[:, grp * LANES:(grp + 1) * LANES]).reshape(hpp, nr, nk)
        return s if bias is None else s + bias[None]

    def softmax(first, s):
        mx = jnp.max(s, axis=-1, keepdims=True)
        if sink_ref is not None:
            hidx = lax.broadcasted_iota(jnp.int32, (hpp, 1, 1), 0)
            sink = jnp.zeros((hpp, 1, 1), F32)
            for i in range(hpp):
                sink = jnp.where(hidx == i, sink_ref[0, first + i] * LOG2E, sink)
            mx = jnp.maximum(mx, sink)
        p = jnp.exp2(s - mx)
        den = jnp.sum(p, axis=-1, keepdims=True)
        if sink_ref is not None:
            den = den + jnp.exp2(sink - mx)
        return p.astype(BF16).reshape(hpp * nr, nk), den

    def weighted_values(first, p, den):
        grp = first // gsz
        o = _mm(p, v[:, grp * LANES:(grp + 1) * LANES]).reshape(hpp, nr, LANES) / den
        return [o[i] for i in range(hpp)]

    passes = list(range(0, N_HEADS, hpp))
    n_pass = len(passes)
    outs = []
    s_buf = {i: scores(passes[i]) for i in range(min(SCORE_LOOKAHEAD, n_pass))}
    for i in range(n_pass):
        if i + SCORE_LOOKAHEAD < n_pass:
            s_buf[i + SCORE_LOOKAHEAD] = scores(passes[i + SCORE_LOOKAHEAD])
        outs += weighted_values(passes[i], *softmax(passes[i], s_buf.pop(i)))
    for j in range(N_HEADS // 2):
        o_ref[rows, j * LANES:(j + 1) * LANES] = jnp.where(upper, outs[2 * j + 1], outs[2 * j]).astype(BF16)


def _window_attn_kernel(*refs, n_kblk, t_seq):
    q_ref, kc_ref, vc_ref = refs[:3]
    kw = refs[3:3 + n_kblk]
    vw = refs[3 + n_kblk:3 + 2 * n_kblk]
    sink_ref, o_ref = refs[3 + 2 * n_kblk:]
    qb = q_ref.shape[0]
    t_ctx = kc_ref.shape[0]
    k = jnp.concatenate([kc_ref[...]] + [r[...] for r in kw], axis=0)
    v = jnp.concatenate([vc_ref[...]] + [r[...] for r in vw], axis=0)
    nk = k.shape[0]
    row = lax.broadcasted_iota(jnp.int32, (qb, nk), 0)
    col = lax.broadcasted_iota(jnp.int32, (qb, nk), 1)
    rel = col - t_ctx - WINDOW
    pos = rel + pl.program_id(1) * qb
    visible = (col < t_ctx) | ((jnp.abs(rel - row) <= WINDOW) & (pos >= 0) & (pos < t_seq))
    bias = jnp.where(visible, 0.0, -jnp.inf)
    _attend(q_ref, slice(0, qb), k, v, bias, sink_ref, o_ref)


def _window_attention(q, k, v, k_ctx, v_ctx, sink, bsz, t_seq, t_ctx):
    m = q.shape[0]
    qb = _pick_tile(t_seq, (256, 128))
    nq = t_seq // qb
    kblk_per_q = qb // WINDOW
    n_kblk = kblk_per_q + 2
    nkb_seq = t_seq // WINDOW
    kvw = k.shape[1]

    def win_spec(j):
        def idx(b, i):
            blk = jnp.clip(i * kblk_per_q + (j - 1), 0, nkb_seq - 1)
            return (b * nkb_seq + blk, 0)
        return pl.BlockSpec((WINDOW, kvw), idx)

    ctx_spec = pl.BlockSpec((t_ctx, kvw), lambda b, i: (b, 0))
    in_specs = ([pl.BlockSpec((qb, ATT_WIDTH), lambda b, i: (b * nq + i, 0)), ctx_spec, ctx_spec]
                + [win_spec(j) for j in range(n_kblk)] * 2
                + [pl.BlockSpec(memory_space=pltpu.SMEM)])
    return pl.pallas_call(
        functools.partial(_window_attn_kernel, n_kblk=n_kblk, t_seq=t_seq),
        grid=(bsz, nq),
        in_specs=in_specs,
        out_specs=pl.BlockSpec((qb, ATT_WIDTH), lambda b, i: (b * nq + i, 0)),
        out_shape=jax.ShapeDtypeStruct((m, ATT_WIDTH), BF16),
        compiler_params=_cparams(2),
        name="window_attn",
    )(q, k_ctx, v_ctx, *([k] * n_kblk), *([v] * n_kblk), sink)


def _dense_attn_kernel(*refs, n_kv_src, has_sink):
    q_ref = refs[0]
    k_refs = refs[1:1 + n_kv_src]
    v_refs = refs[1 + n_kv_src:1 + 2 * n_kv_src]
    rest = refs[1 + 2 * n_kv_src:]
    sink_ref = rest[0] if has_sink else None
    o_ref = rest[-1]
    if n_kv_src == 1:
        k, v = k_refs[0][...], v_refs[0][...]
    else:
        k = jnp.concatenate([r[...] for r in k_refs], axis=0)
        v = jnp.concatenate([r[...] for r in v_refs], axis=0)
    _attend(q_ref, slice(0, q_ref.shape[0]), k, v, None, sink_ref, o_ref)


def _dense_attention(q, ks, vs, sink, bsz, t_q):
    m = q.shape[0]
    qb = _pick_tile(t_q, (512, 256, 128))
    nq = t_q // qb
    kv_specs = [pl.BlockSpec((a.shape[0] // bsz, a.shape[1]), lambda b, i: (b, 0)) for a in ks]
    in_specs = [pl.BlockSpec((qb, ATT_WIDTH), lambda b, i: (b * nq + i, 0))] + kv_specs * 2
    args = [q, *ks, *vs]
    if sink is not None:
        in_specs.append(pl.BlockSpec(memory_space=pltpu.SMEM))
        args.append(sink)
    return pl.pallas_call(
        functools.partial(_dense_attn_kernel, n_kv_src=len(ks), has_sink=sink is not None),
        grid=(bsz, nq),
        in_specs=in_specs,
        out_specs=pl.BlockSpec((qb, ATT_WIDTH), lambda b, i: (b * nq + i, 0)),
        out_shape=jax.ShapeDtypeStruct((m, ATT_WIDTH), BF16),
        compiler_params=_cparams(2),
        name="dense_attn",
    )(*args)


def _split3(x):
    hi = x.astype(BF16)
    r1 = x - hi.astype(F32)
    mid = r1.astype(BF16)
    lo = (r1 - mid.astype(F32)).astype(BF16)
    return hi, mid, lo


GATE_QUANTITIES = 5
GQ_A, GQ_INTER, GQ_NRM, GQ_END, GQ_DECAY = range(GATE_QUANTITIES)


def _mlstm_gate_kernel(gt_ref, gbt_ref, m0_ref, o_ref, rpc_ref, m_out, m_scr, *, rev):
    gi = pl.program_id(0)
    n_g = pl.num_programs(0)
    bsz, _, blk = gt_ref.shape
    rows = bsz * N_GATES
    n_chunks = blk // CHUNK

    @pl.when(gi == 0)
    def _():
        m_scr[...] = m0_ref[...]

    gates_t = (gt_ref[...] + gbt_ref[...]).reshape(rows, blk)
    r = lax.broadcasted_iota(jnp.int32, (blk, blk), 0)
    c = lax.broadcasted_iota(jnp.int32, (blk, blk), 1)
    feeds_b = jnp.where((r >= c) if rev else (r <= c), 1.0, 0.0).astype(BF16)
    brow = sum(_mm(piece, feeds_b) for piece in _split3(_log_sigmoid(gates_t)))
    b_all = pltpu.roll(brow, rows - C_HEADS, 0)
    rp = gates_t - b_all
    lane = lax.broadcasted_iota(jnp.int32, (rows, blk), 1)
    pos = lane % CHUNK
    seg = rp
    shift = 1
    while shift < CHUNK:
        if rev:
            seg = jnp.where(pos < CHUNK - shift, jnp.maximum(seg, pltpu.roll(seg, blk - shift, 1)), seg)
        else:
            seg = jnp.where(pos >= shift, jnp.maximum(seg, pltpu.roll(seg, shift, 1)), seg)
        shift *= 2
    m_init = m_scr[:, 0:1]
    m_prev = m_init
    b_off = jnp.zeros((rows, 1), F32)
    m_lane = jnp.zeros((rows, blk), F32)
    off_lane = jnp.zeros((rows, blk), F32)
    for ch in (range(n_chunks - 1, -1, -1) if rev else range(n_chunks)):
        end = ch * CHUNK if rev else (ch + 1) * CHUNK - 1
        here = (lane // CHUNK) == ch
        m_lane = jnp.where(here, m_prev, m_lane)
        off_lane = jnp.where(here, b_off, off_lane)
        b_end = b_all[:, end:end + 1]
        m_prev = (b_end - b_off) + jnp.maximum(m_prev, b_off + seg[:, end:end + 1])
        b_off = b_end
    m_last, b_last = m_prev, b_off
    a = off_lane - jnp.maximum(m_lane, off_lane + seg)
    quantities = [None] * GATE_QUANTITIES
    quantities[GQ_A] = a
    quantities[GQ_INTER] = jnp.exp(a + m_init)
    quantities[GQ_NRM] = jnp.exp(a - b_all)
    quantities[GQ_END] = jnp.exp(rp + (b_last - m_last))
    quantities[GQ_DECAY] = jnp.broadcast_to(jnp.exp(b_last + m_init - m_last), (rows, blk))
    o_ref[...] = jnp.concatenate([x.reshape(bsz, N_GATES, blk) for x in quantities], axis=1)
    rp_t = rp.T
    for b in range(bsz):
        rpc_ref[b] = rp_t[:, b * N_GATES:(b + 1) * N_GATES]
    m_scr[...] = jnp.broadcast_to(m_last, (rows, LANES))

    @pl.when(gi == n_g - 1)
    def _():
        m_out[...] = m_scr[...]


def _mlstm_gates(gt, gate_b, m_state, t_seq, rev):
    bsz = gt.shape[0]
    blk = _pick_tile(t_seq, (256, 128))
    ng = t_seq // blk
    step = (lambda i: ng - 1 - i) if rev else (lambda i: i)
    tok = lambda i: (0, 0, step(i))
    return pl.pallas_call(
        functools.partial(_mlstm_gate_kernel, rev=rev),
        grid=(ng,),
        in_specs=[pl.BlockSpec((bsz, N_GATES, blk), tok), _resident((1, N_GATES, 1)), _resident(m_state.shape)],
        out_specs=[pl.BlockSpec((bsz, GATE_QUANTITIES * N_GATES, blk), tok),
                   pl.BlockSpec((bsz, blk, N_GATES), lambda i: (0, step(i), 0)),
                   _resident(m_state.shape)],
        out_shape=[jax.ShapeDtypeStruct((bsz, GATE_QUANTITIES * N_GATES, t_seq), F32),
                   jax.ShapeDtypeStruct((bsz, t_seq, N_GATES), F32),
                   jax.ShapeDtypeStruct(m_state.shape, F32)],
        scratch_shapes=[pltpu.VMEM(m_state.shape, F32)],
        compiler_params=_cparams(1),
        name="mlstm_gates_bwd" if rev else "mlstm_gates_fwd",
    )(gt, gate_b.reshape(1, N_GATES, 1), m_state)


N_DIR = 2


def _mlstm_kernel(*refs):
    per_dir = [refs[0:6], refs[6:12]]
    h_refs, s_outs, s_scr = refs[12:14], refs[14:16], refs[16]
    gi = pl.program_id(1)
    n_g = pl.num_programs(1)
    blk = refs[0].shape[0]

    @pl.when(gi == 0)
    def _():
        for d in range(N_DIR):
            s_scr[d] = per_dir[d][5][0]

    r = lax.broadcasted_iota(jnp.int32, (blk, blk), 0)
    c = lax.broadcasted_iota(jnp.int32, (blk, blk), 1)
    ones = jnp.ones((C_HEAD_DIM, blk), BF16)
    units = []
    for d, rev in enumerate((False, True)):
        k_ref, qt_ref, vt_ref, rpc_ref, gq_ref, _ = per_dir[d]
        feeds = (r >= c) if rev else (r <= c)
        base = 2 * C_HEADS if rev else 0
        rp_col = rpc_ref[0][:, base:base + C_HEADS]
        gq = gq_ref[0]
        for hd in range(C_HEADS):
            lanes = slice(hd * C_HEAD_DIM, (hd + 1) * C_HEAD_DIM)
            rows = [gq[q * N_GATES + base + hd:q * N_GATES + base + hd + 1] for q in range(GATE_QUANTITIES)]
            units.append(dict(
                d=d, hd=hd, lanes=lanes, feeds=feeds, rows=rows, rp=rp_col[:, hd:hd + 1],
                kh=k_ref[:, lanes],
                qth=qt_ref[lanes, :],
                v_aug=jnp.concatenate([vt_ref[lanes, :], ones], axis=0),
                st=s_scr[d, hd]))
    for u in units:
        u["s"] = _mm(u["kh"], u["qth"])
        u["inter"] = _mm(u["st"].astype(BF16), u["qth"])
        x_t = (u["rows"][GQ_END] * u["v_aug"].astype(F32)).astype(BF16)
        u["st_new"] = u["rows"][GQ_DECAY][:, :C_HEAD_DIM] * u["st"] + _mm(x_t, u["kh"])
    for u in units:
        w_t = jnp.exp(jnp.where(u["feeds"], u["rp"] + u["rows"][GQ_A], -jnp.inf))
        u["p"] = (u["s"] * w_t).astype(BF16)
    for u in units:
        tot = _mm(u["v_aug"], u["p"]) + u["rows"][GQ_INTER] * u["inter"]
        u["h_t"] = tot[:C_HEAD_DIM] / jnp.maximum(jnp.abs(tot[C_HEAD_DIM:]), u["rows"][GQ_NRM])
    for u in units:
        h_refs[u["d"]][:, u["lanes"]] = u["h_t"].T
        s_scr[u["d"], u["hd"]] = u["st_new"]

    @pl.when(gi == n_g - 1)
    def _():
        for d in range(N_DIR):
            s_outs[d][0] = s_scr[d]


def _mlstm_scan(k, qt, vt, rpcs, gqs, states, bsz, t_seq):
    m = k.shape[0]
    blk = _pick_tile(t_seq, (256, 128))
    ng = t_seq // blk
    st_spec = pl.BlockSpec((1, C_HEADS, 2 * C_HEAD_DIM, C_HEAD_DIM), lambda b, i: (b, 0, 0, 0))
    in_specs, args, h_specs = [], [], []
    for d, step in enumerate((lambda i: i, lambda i: ng - 1 - i)):
        tok = lambda b, i, step=step: (b * ng + step(i), 0)
        tok_t = lambda b, i, step=step: (0, b * ng + step(i))
        in_specs += [
            pl.BlockSpec((blk, C_WIDTH), tok),
            pl.BlockSpec((C_WIDTH, blk), tok_t),
            pl.BlockSpec((C_WIDTH, blk), tok_t),
            pl.BlockSpec((1, blk, N_GATES), lambda b, i, step=step: (b, step(i), 0)),
            pl.BlockSpec((1, GATE_QUANTITIES * N_GATES, blk), lambda b, i, step=step: (b, 0, step(i))),
            st_spec,
        ]
        args += [k, qt, vt, rpcs[d], gqs[d], states[d]]
        h_specs.append(pl.BlockSpec((blk, C_WIDTH), tok))
    return pl.pallas_call(
        _mlstm_kernel,
        grid=(bsz, ng),
        in_specs=in_specs,
        out_specs=h_specs + [st_spec] * N_DIR,
        out_shape=[jax.ShapeDtypeStruct((m, C_WIDTH), F32)] * N_DIR
                  + [jax.ShapeDtypeStruct(states[0].shape, F32)] * N_DIR,
        scratch_shapes=[pltpu.VMEM((N_DIR, C_HEADS, 2 * C_HEAD_DIM, C_HEAD_DIM), F32)],
        compiler_params=_cparams(2),
        name="mlstm_scan",
    )(*args)


MERGE_COL_CHUNK = 256


def _merge_kernel(x_ref, mod_ref, ng_ref, ya_ref, yb_ref, hf_ref, hb_ref, om_ref, hg_ref,
                  w_gl, w_br, w_out, o_ref, acc_scr):
    x = x_ref[...]
    mod = mod_ref[0]
    ng = ng_ref[...]
    h = _mod_norm(x, mod, ng, 1).astype(BF16)
    d = x.shape[1]
    hsum = hf_ref[...] + hb_ref[...]
    om = om_ref[...]
    ym = []
    for j in range(C_HEADS):
        lanes = slice(j * C_HEAD_DIM, (j + 1) * C_HEAD_DIM)
        ym.append((_sigmoid(om[:, lanes]) * _rms(hsum[:, lanes], hg_ref[:, lanes])).astype(BF16))
    ys = [ya_ref[...], yb_ref[...], jnp.concatenate(ym, axis=1)]
    n_chunks = d // MERGE_COL_CHUNK

    def products(c):
        cols = slice(c * MERGE_COL_CHUNK, (c + 1) * MERGE_COL_CHUNK)
        return [(_mm(h, w_gl[:, n * d + cols.start:n * d + cols.stop]), _mm(ys[n], w_br[n, :, cols]))
                for n in range(N_BRANCH)]

    nxt = products(0)
    for c in range(n_chunks):
        cur = nxt
        if c + 1 < n_chunks:
            nxt = products(c + 1)
        acc = sum(_sigmoid(gl) * z for gl, z in cur)
        acc_scr[:, c * MERGE_COL_CHUNK:(c + 1) * MERGE_COL_CHUNK] = acc.astype(BF16)
    y = _mm(acc_scr[...], w_out[...])
    o_ref[...] = _gated_residual(x, y, mod, ng, 1, 1.0)


def _merge(x, mod, ng, ya, yb, hf, hb, om, head_g, w_gl, w_br, w_out, rows_per_mod, mod_base):
    m, d = x.shape
    tm = _pick_tile(rows_per_mod)
    tiles_per_mod = rows_per_mod // tm
    row = lambda i: (i, 0)
    return pl.pallas_call(
        _merge_kernel,
        grid=(m // tm,),
        in_specs=[
            pl.BlockSpec((tm, d), row),
            _mod_spec(mod, lambda i: mod_base + i // tiles_per_mod),
            _resident_slab(ng),
            pl.BlockSpec((tm, ATT_WIDTH), row),
            pl.BlockSpec((tm, ATT_WIDTH), row),
            pl.BlockSpec((tm, C_WIDTH), row),
            pl.BlockSpec((tm, C_WIDTH), row),
            pl.BlockSpec((tm, C_WIDTH), row),
            _resident_slab(head_g),
            _resident_slab(w_gl),
            _resident_slab(w_br),
            _resident_slab(w_out),
        ],
        out_specs=pl.BlockSpec((tm, d), row),
        out_shape=jax.ShapeDtypeStruct((m, d), F32),
        scratch_shapes=[pltpu.VMEM((tm, d), BF16)],
        compiler_params=_cparams(1),
        name="merge",
    )(x, mod.arr, ng.arr, ya, yb, hf, hb, om, head_g.arr, w_gl.arr, w_br.arr, w_out.arr)


def _rope_tables(t_seq):
    rows = t_seq // GRID_W
    row = jnp.repeat(jnp.arange(rows), GRID_W)
    col = jnp.tile(jnp.arange(GRID_W), rows)
    freqs = ROPE_THETA ** (-jnp.arange(ROPE_PAIRS, dtype=F32) / ROPE_PAIRS)
    ang = jnp.stack([row[:, None] * freqs, col[:, None] * freqs], axis=1)
    cos, sin = jnp.cos(ang), jnp.sin(ang)
    cos_head = jnp.concatenate([cos[:, 0], cos[:, 0], cos[:, 1], cos[:, 1]], axis=-1)
    sin_head = jnp.concatenate([-sin[:, 0], sin[:, 0], -sin[:, 1], sin[:, 1]], axis=-1)
    return jnp.tile(cos_head, (1, 2)), jnp.tile(sin_head, (1, 2))


def _split_in_weights(w_in):
    sizes = (ATT_WIDTH, N_KV * HEAD_DIM, N_KV * HEAD_DIM, ATT_WIDTH, N_KV * HEAD_DIM, N_KV * HEAD_DIM,
             C_WIDTH, C_WIDTH, C_WIDTH, C_WIDTH, N_GATES, N_BRANCH * w_in.shape[1])
    parts, start = [], 0
    for s in sizes:
        parts.append(w_in[..., start:start + s])
        start += s
    qa, ka, va, qb, kb, vb, qm, km, vm, om, gm, gl = parts
    cat = lambda ws: jnp.concatenate(ws, axis=-1)

    def doubled(w):
        return cat([w[..., :HEAD_DIM], w[..., :HEAD_DIM], w[..., HEAD_DIM:], w[..., HEAD_DIM:]])

    bf = lambda w: w.astype(BF16)
    weights = [bf(qa), bf(cat([doubled(ka), doubled(va)])), bf(qb), bf(cat([doubled(kb), doubled(vb)])),
               bf(cat([qm, km])), bf(jnp.swapaxes(cat([vm, gm]), -1, -2)), bf(om)]
    return weights, bf(gl)


def kernel(x, c, ctx, c_ctx, w_ada, b_ada, norm_g, ffn_w_gate, ffn_w_up, ffn_w_down, w_in, attn_sink,
           qk_norm_g, conv_w, conv_b, mlstm_gate_b, mlstm_norm_g, w_branch, w_out):
    bsz, t_lat, d = x.shape
    t_ctx = ctx.shape[1]
    depth = w_ada.shape[0]
    xs = x.reshape(bsz * t_lat, d)
    cs = ctx.reshape(bsz * t_ctx, d)
    mods = _modulation(jnp.concatenate([c, c_ctx[None]], axis=0), w_ada, b_ada)
    rope_tabs = _rope_tables(t_lat)
    zero_state = jnp.zeros((bsz, C_HEADS, 2 * C_HEAD_DIM, C_HEAD_DIM), F32)
    zero_m = jnp.zeros((bsz * N_GATES, LANES), F32)

    ffn_all = [w.astype(BF16) for w in (ffn_w_gate, ffn_w_up, ffn_w_down)]
    in_w_all, w_gl_all = _split_in_weights(w_in)
    w_br_all, w_o_all = w_branch.astype(BF16), w_out.astype(BF16)
    qkn_all = jnp.tile(qk_norm_g, (1, 1, 2))
    conv_b_all = conv_b[:, None, :]
    head_g_all = mlstm_norm_g[:, None, :]

    for l in range(depth):
        last = l == depth - 1
        mod, ng = _Slab(mods, l), _Slab(norm_g, l)
        ffn = [[_Slab(w, l, k) for w in ffn_all] for k in range(2)]
        in_w = [_Slab(w, l) for w in in_w_all]
        w_gl, w_br, w_o = _Slab(w_gl_all, l), _Slab(w_br_all, l), _Slab(w_o_all, l)
        qkn, cw, cb, head_g = _Slab(qkn_all, l), _Slab(conv_w, l), _Slab(conv_b_all, l), _Slab(head_g_all, l)
        sink = attn_sink[l].reshape(1, N_HEADS)
        lat = dict(rows_per_mod=t_lat, mod_base=0)
        con = dict(rows_per_mod=bsz * t_ctx, mod_base=bsz)

        xs = _half_ffn(xs, mod, ng, *ffn[0], 0, **lat)
        cs = _half_ffn(cs, mod, ng, *ffn[0], 0, **con)

        (qa, ka, va, qb, kb, vb, qmt, km, vm, om, gmt) = _in_projection(
            xs, mod, ng, rope_tabs, qkn, in_w, cw, cb, t_seq=t_lat, **lat)
        (qa_c, ka_c, va_c, qb_c, kb_c, vb_c, qmt_c, km_c, vm_c, om_c, gmt_c) = _in_projection(
            cs, mod, ng, None, qkn, in_w, cw, cb, t_seq=t_ctx, **con)

        ya = _window_attention(qa, ka, va, ka_c, va_c, sink, bsz, t_lat, t_ctx)
        yb = _dense_attention(qb, [kb_c, kb], [vb_c, vb], None, bsz, t_lat)

        gb = mlstm_gate_b[l]
        gq_cf, rp_cf, m_f = _mlstm_gates(gmt_c, gb, zero_m, t_ctx, False)
        gq_cb, rp_cb, m_b = _mlstm_gates(gmt_c, gb, zero_m, t_ctx, True)
        gq_f, rp_f, _ = _mlstm_gates(gmt, gb, m_f, t_lat, False)
        gq_b, rp_b, _ = _mlstm_gates(gmt, gb, m_b, t_lat, True)
        hf_c, hb_c, st_f, st_b = _mlstm_scan(km_c, qmt_c, vm_c, (rp_cf, rp_cb), (gq_cf, gq_cb),
                                             (zero_state, zero_state), bsz, t_ctx)
        hf, hb, _, _ = _mlstm_scan(km, qmt, vm, (rp_f, rp_b), (gq_f, gq_b), (st_f, st_b), bsz, t_lat)

        xs = _merge(xs, mod, ng, ya, yb, hf, hb, om, head_g, w_gl, w_br, w_o, **lat)
        xs = _half_ffn(xs, mod, ng, *ffn[1], 2, **lat)
        if not last:
            ya_c = _dense_attention(qa_c, [ka_c], [va_c], sink, bsz, t_ctx)
            yb_c = _dense_attention(qb_c, [kb_c], [vb_c], None, bsz, t_ctx)
            cs = _merge(cs, mod, ng, ya_c, yb_c, hf_c, hb_c, om_c, head_g, w_gl, w_br, w_o, **con)
            cs = _half_ffn(cs, mod, ng, *ffn[1], 2, **con)
    return xs.reshape(bsz, t_lat, d)
```

```python
import functools

import jax
import jax.numpy as jnp
from jax import lax
from jax.experimental import pallas as pl
from jax.experimental.pallas import tpu as pltpu

F32 = jnp.float32
BF16 = jnp.bfloat16

EPS = 1e-6
N_MOD = 9
GRID_W = 64
ROPE_THETA = 10000.0
HEAD_DIM = 64
ROPE_PAIRS = HEAD_DIM // 4
N_HEADS = 8
N_KV = 2
ATT_WIDTH = N_HEADS * HEAD_DIM
WINDOW = 128
C_HEADS = 4
C_HEAD_DIM = 128
C_WIDTH = C_HEADS * C_HEAD_DIM
C_CONV = 5
CHUNK = 64
N_GATES = 4 * C_HEADS
N_BRANCH = 3
LOG2E = 1.4426950408889634

LANES = 128
SUBLANES = 8
VMEM_LIMIT_BYTES = 56 * 1024 * 1024


def _cparams(n_axes):
    return pltpu.CompilerParams(
        dimension_semantics=("arbitrary",) * n_axes,
        vmem_limit_bytes=VMEM_LIMIT_BYTES,
    )


def _pick_tile(n, candidates=(512, 256, 128)):
    for c in candidates:
        if n % c == 0:
            return c
    raise ValueError(f"no tile size for {n}")


def _resident(shape):
    zeros = (0,) * len(shape)
    return pl.BlockSpec(shape, lambda *_: zeros, pipeline_mode=pl.Buffered(1))


class _Slab:
    def __init__(self, arr, *idx):
        self.arr, self.idx = arr, idx
        self.shape = arr.shape[len(idx):]


def _resident_slab(slab):
    index = slab.idx + (0,) * len(slab.shape)
    return pl.BlockSpec((None,) * len(slab.idx) + slab.shape, lambda *_: index, pipeline_mode=pl.Buffered(1))


def _mod_spec(mod, row_of_step):
    lead = mod.idx
    return pl.BlockSpec((None,) * len(lead) + (1,) + mod.shape[1:], lambda i: lead + (row_of_step(i), 0, 0))


def _mm(a, b):
    return jnp.dot(a, b, preferred_element_type=F32)


def _mm_nt(a, b):
    return lax.dot_general(a, b, (((1,), (1,)), ((), ())), preferred_element_type=F32)


def _rms(x, g):
    return x * lax.rsqrt(jnp.mean(x * x, axis=-1, keepdims=True) + EPS) * g


def _sigmoid(x):
    return 1.0 / (1.0 + jnp.exp(-x))


def _silu(x):
    return x * _sigmoid(x)


def _log_sigmoid(x):
    return jnp.minimum(x, 0.0) - jnp.log(1.0 + jnp.exp(-jnp.abs(x)))


def _mod_norm(x, mod, ng, k):
    return _rms(x, ng[2 * k:2 * k + 1]) * (1.0 + mod[3 * k + 1:3 * k + 2]) + mod[3 * k:3 * k + 1]


def _gated_residual(x, y, mod, ng, k, weight):
    return x + (weight * mod[3 * k + 2:3 * k + 3]) * _rms(y, ng[2 * k + 1:2 * k + 2])


def _mod_kernel(c_ref, w_ref, b_ref, o_ref):
    a = _silu(c_ref[...]).astype(BF16)
    o_ref[0] = _mm(a, w_ref[0].astype(BF16)) + b_ref[0]


def _modulation(c_all, w_ada, b_ada):
    n_layers, d, nd = w_ada.shape
    r = c_all.shape[0]
    out = pl.pallas_call(
        _mod_kernel,
        grid=(n_layers, nd // d),
        in_specs=[
            pl.BlockSpec((r, d), lambda l, j: (0, 0)),
            pl.BlockSpec((1, d, d), lambda l, j: (l, 0, j)),
            pl.BlockSpec((1, 1, d), lambda l, j: (l, 0, j)),
        ],
        out_specs=pl.BlockSpec((1, r, d), lambda l, j: (l, 0, j)),
        out_shape=jax.ShapeDtypeStruct((n_layers, r, nd), F32),
        compiler_params=_cparams(2),
        name="adaln_mod",
    )(c_all, w_ada, b_ada.reshape(n_layers, 1, nd))
    return out.reshape(n_layers, r, N_MOD, d)


FFN_COL_CHUNK = 256


def _ffn_kernel(x_ref, mod_ref, ng_ref, wg_ref, wu_ref, wd_ref, o_ref, a_scr, *, k):
    x = x_ref[...]
    mod = mod_ref[0]
    ng = ng_ref[...]
    h = _mod_norm(x, mod, ng, k).astype(BF16)
    n_chunks = wg_ref.shape[1] // FFN_COL_CHUNK
    cols = [slice(j * FFN_COL_CHUNK, (j + 1) * FFN_COL_CHUNK) for j in range(n_chunks)]
    gate_up = lambda j: (_mm(h, wg_ref[:, cols[j]]), _mm(h, wu_ref[:, cols[j]]))
    nxt = gate_up(0)
    for j in range(n_chunks):
        g, u = nxt
        if j + 1 < n_chunks:
            nxt = gate_up(j + 1)
        a_scr[:, cols[j]] = (_silu(g) * u).astype(BF16)
    y = _mm(a_scr[...], wd_ref[...])
    o_ref[...] = _gated_residual(x, y, mod, ng, k, 0.5)


def _half_ffn(x, mod, ng, wg, wu, wd, k, rows_per_mod, mod_base):
    m, d = x.shape
    d_ff = wg.shape[1]
    tm = _pick_tile(rows_per_mod, (1024, 512, 256, 128))
    tiles_per_mod = rows_per_mod // tm
    slabs = [ng, wg, wu, wd]
    return pl.pallas_call(
        functools.partial(_ffn_kernel, k=k),
        grid=(m // tm,),
        in_specs=[pl.BlockSpec((tm, d), lambda i: (i, 0)),
                  _mod_spec(mod, lambda i: mod_base + i // tiles_per_mod)] + [_resident_slab(s) for s in slabs],
        out_specs=pl.BlockSpec((tm, d), lambda i: (i, 0)),
        out_shape=jax.ShapeDtypeStruct((m, d), F32),
        scratch_shapes=[pltpu.VMEM((tm, d_ff), BF16)],
        compiler_params=_cparams(1),
        name=f"half_ffn{k}",
    )(x, mod.arr, *[s.arr for s in slabs])


def _head_rms(x, gain):
    low = lax.broadcasted_iota(jnp.int32, x.shape, 1) < HEAD_DIM
    ss = x * x
    s_lo = jnp.sum(jnp.where(low, ss, 0.0), axis=-1, keepdims=True)
    s_hi = jnp.sum(jnp.where(low, 0.0, ss), axis=-1, keepdims=True)
    inv = jnp.where(low, lax.rsqrt(s_lo * (1.0 / HEAD_DIM) + EPS), lax.rsqrt(s_hi * (1.0 / HEAD_DIM) + EPS))
    return x * inv * gain


def _rope(x, cos, sin_signed):
    lane = lax.broadcasted_iota(jnp.int32, x.shape, 1)
    first_half = (lane & ROPE_PAIRS) == 0
    partner = jnp.where(first_half, pltpu.roll(x, LANES - ROPE_PAIRS, 1), pltpu.roll(x, ROPE_PAIRS, 1))
    return x * cos + partner * sin_signed


CONV_HALO = SUBLANES


def _inproj_kernel(*refs, rope, tiles_per_seq):
    x_ref, prev_ref, next_ref, mod_ref, ng_ref = refs[:5]
    refs = refs[5:]
    if rope:
        cos, sin = refs[0][...], refs[1][...]
        refs = refs[2:]
    (qkn_ref, w_qa, w_kva, w_qb, w_kvb, w_qkm, w_vgt, w_om, cw_ref, cb_ref,
     qa_ref, ka_ref, va_ref, qb_ref, kb_ref, vb_ref, qmt_ref, km_ref, vm_ref, om_ref, gt_ref, xp) = refs
    tm = x_ref.shape[0]
    mod, ng = mod_ref[0], ng_ref[...]
    h = _mod_norm(x_ref[...], mod, ng, 1).astype(BF16)
    qn, kn = qkn_ref[0:1], qkn_ref[1:2]
    kv_w = 2 * N_KV * HEAD_DIM

    def finish(full, out_ref, width, norm_gain, scale):
        for j in range(width // LANES):
            blk = full[:, j * LANES:(j + 1) * LANES]
            if norm_gain is not None:
                blk = _head_rms(blk, norm_gain)
            if rope:
                blk = _rope(blk, cos, sin)
            if scale != 1.0:
                blk = blk * scale
            out_ref[:, j * LANES:(j + 1) * LANES] = blk.astype(BF16)

    q_scale = HEAD_DIM ** -0.5 * LOG2E
    halo = jnp.concatenate([prev_ref[...], next_ref[...]], axis=0)
    h_halo = _mod_norm(halo, mod, ng, 1).astype(BF16)
    qa = _mm(h, w_qa[...])
    kva = _mm(h, w_kva[...])
    finish(qa, qa_ref, ATT_WIDTH, None, q_scale)
    qb = _mm(h, w_qb[...])
    finish(kva, ka_ref, kv_w, None, 1.0)
    va_ref[...] = kva[:, kv_w:].astype(BF16)
    kvb = _mm(h, w_kvb[...])
    finish(qb, qb_ref, ATT_WIDTH, qn, q_scale)
    qk = _mm(jnp.concatenate([h, h_halo], axis=0), w_qkm[...])
    finish(kvb, kb_ref, kv_w, kn, 1.0)
    vb_ref[...] = kvb[:, kv_w:].astype(BF16)
    vgt = _mm_nt(w_vgt[...], h)
    vm_ref[...] = vgt[:C_WIDTH].astype(BF16)
    gt_ref[0] = vgt[C_WIDTH:]
    om_ref[...] = _mm(h, w_om[...])

    ti = pl.program_id(0) % tiles_per_seq
    zero = jnp.zeros((CONV_HALO, qk.shape[1]), F32)
    xp[0:CONV_HALO] = jnp.where(ti > 0, qk[tm:tm + CONV_HALO], zero)
    xp[CONV_HALO:CONV_HALO + tm] = qk[:tm]
    xp[CONV_HALO + tm:] = jnp.where(ti < tiles_per_seq - 1, qk[tm + CONV_HALO:], zero)
    first = CONV_HALO - C_CONV // 2
    acc = cb_ref[...] + cw_ref[0:1] * xp[first:first + tm]
    for j in range(1, C_CONV):
        acc = acc + cw_ref[j:j + 1] * xp[first + j:first + j + tm]
    y = _silu(acc)
    qmt_ref[...] = (y[:, :C_WIDTH] * (C_HEAD_DIM ** -0.5)).T.astype(BF16)
    km_ref[...] = y[:, C_WIDTH:].astype(BF16)


def _in_projection(x, mod, ng, rope_tabs, qkn, weights, conv_w, conv_b, rows_per_mod, mod_base, t_seq):
    m, d = x.shape
    tm = _pick_tile(min(rows_per_mod, t_seq), (1024, 512, 256, 128))
    tiles_per_mod = rows_per_mod // tm
    tiles_per_seq = t_seq // tm
    rope = rope_tabs is not None
    row = lambda i: (i, 0)
    hb = tm // CONV_HALO
    n_hb = m // CONV_HALO
    in_specs = [
        pl.BlockSpec((tm, d), row),
        pl.BlockSpec((CONV_HALO, d), lambda i: (jnp.maximum(i * hb - 1, 0), 0)),
        pl.BlockSpec((CONV_HALO, d), lambda i: (jnp.minimum((i + 1) * hb, n_hb - 1), 0)),
        _mod_spec(mod, lambda i: mod_base + i // tiles_per_mod),
        _resident_slab(ng),
    ]
    args = [x, x, x, mod.arr, ng.arr]
    if rope:
        in_specs += [pl.BlockSpec((tm, LANES), lambda i: (i % tiles_per_seq, 0))] * 2
        args += list(rope_tabs)
    slabs = [qkn] + list(weights) + [conv_w, conv_b]
    in_specs += [_resident_slab(s) for s in slabs]
    args += [s.arr for s in slabs]
    kv_w = 2 * N_KV * HEAD_DIM
    col = lambda i: (0, i)
    outs = [(ATT_WIDTH, BF16, False), (kv_w, BF16, False), (kv_w, BF16, False),
            (ATT_WIDTH, BF16, False), (kv_w, BF16, False), (kv_w, BF16, False),
            (C_WIDTH, BF16, True), (C_WIDTH, BF16, False), (C_WIDTH, BF16, True), (C_WIDTH, F32, False)]
    out_specs = [pl.BlockSpec((w, tm), col) if fm else pl.BlockSpec((tm, w), row) for w, _, fm in outs]
    out_shape = [jax.ShapeDtypeStruct((w, m) if fm else (m, w), dt) for w, dt, fm in outs]
    out_specs.append(pl.BlockSpec((1, N_GATES, tm), lambda i: (i // tiles_per_seq, 0, i % tiles_per_seq)))
    out_shape.append(jax.ShapeDtypeStruct((m // t_seq, N_GATES, t_seq), F32))
    return pl.pallas_call(
        functools.partial(_inproj_kernel, rope=rope, tiles_per_seq=tiles_per_seq),
        grid=(m // tm,),
        in_specs=in_specs,
        out_specs=out_specs,
        out_shape=out_shape,
        scratch_shapes=[pltpu.VMEM((tm + 2 * CONV_HALO, 2 * C_WIDTH), F32)],
        compiler_params=_cparams(1),
        name="in_proj_rope" if rope else "in_proj",
    )(*args)


DENSE_SMALL_KEYS = 1024
SCORE_LOOKAHEAD = 1


def _attend(q_ref, rows, k, v, bias, sink_ref, o_ref):
    nr = rows.stop - rows.start
    nk = k.shape[0]
    gsz = N_HEADS // N_KV
    hpp = 2 if nk <= DENSE_SMALL_KEYS else 1
    upper = lax.broadcasted_iota(jnp.int32, (nr, LANES), 1) >= HEAD_DIM

    def scores(first):
        grp = first // gsz
        qs = []
        for hd in range(first, first + hpp):
            qblk = q_ref[rows, (hd // 2) * LANES:(hd // 2 + 1) * LANES]
            mine = upper if hd % 2 else jnp.logical_not(upper)
            qs.append(jnp.where(mine, qblk, jnp.zeros_like(qblk)))
        q_stack = qs[0] if hpp == 1 else jnp.concatenate(qs, axis=0)
        s = _mm_nt(q_stack, k[:, grp * LANES:(grp + 1) * LANES]).reshape(hpp, nr, nk)
        return s if bias is None else s + bias[None]

    def softmax(first, s):
        mx = jnp.max(s, axis=-1, keepdims=True)
        if sink_ref is not None:
            hidx = lax.broadcasted_iota(jnp.int32, (hpp, 1, 1), 0)
            sink = jnp.zeros((hpp, 1, 1), F32)
            for i in range(hpp):
                sink = jnp.where(hidx == i, sink_ref[0, first + i] * LOG2E, sink)
            mx = jnp.maximum(mx, sink)
        p = jnp.exp2(s - mx)
        den = jnp.sum(p, axis=-1, keepdims=True)
        if sink_ref is not None:
            den = den + jnp.exp2(sink - mx)
        return p.astype(BF16).reshape(hpp * nr, nk), den

    def weighted_values(first, p, den):
        grp = first // gsz
        o = _mm(p, v[:, grp * LANES:(grp + 1) * LANES]).reshape(hpp, nr, LANES) / den
        return [o[i] for i in range(hpp)]

    passes = list(range(0, N_HEADS, hpp))
    n_pass = len(passes)
    outs = []
    s_buf = {i: scores(passes[i]) for i in range(min(SCORE_LOOKAHEAD, n_pass))}
    for i in range(n_pass):
        if i + SCORE_LOOKAHEAD < n_pass:
            s_buf[i + SCORE_LOOKAHEAD] = scores(passes[i + SCORE_LOOKAHEAD])
        outs += weighted_values(passes[i], *softmax(passes[i], s_buf.pop(i)))
    for j in range(N_HEADS // 2):
        o_ref[rows, j * LANES:(j + 1) * LANES] = jnp.where(upper, outs[2 * j + 1], outs[2 * j]).astype(BF16)


def _window_attn_kernel(*refs, n_kblk, t_seq):
    q_ref, kc_ref, vc_ref = refs[:3]
    kw = refs[3:3 + n_kblk]
    vw = refs[3 + n_kblk:3 + 2 * n_kblk]
    sink_ref, o_ref = refs[3 + 2 * n_kblk:]
    qb = q_ref.shape[0]
    t_ctx = kc_ref.shape[0]
    nk = t_ctx + 3 * WINDOW
    row = lax.broadcasted_iota(jnp.int32, (WINDOW, nk), 0)
    col = lax.broadcasted_iota(jnp.int32, (WINDOW, nk), 1)
    rel = col - t_ctx - WINDOW
    near = jnp.abs(rel - row) <= WINDOW
    for r in range(qb // WINDOW):
        k = jnp.concatenate([kc_ref[...]] + [kw[r + j][...] for j in range(3)], axis=0)
        v = jnp.concatenate([vc_ref[...]] + [vw[r + j][...] for j in range(3)], axis=0)
        pos = rel + (pl.program_id(1) * qb + r * WINDOW)
        visible = (col < t_ctx) | (near & (pos >= 0) & (pos < t_seq))
        bias = jnp.where(visible, 0.0, -jnp.inf)
        _attend(q_ref, slice(r * WINDOW, (r + 1) * WINDOW), k, v, bias, sink_ref, o_ref)


def _window_attention(q, k, v, k_ctx, v_ctx, sink, bsz, t_seq, t_ctx):
    m = q.shape[0]
    qb = _pick_tile(t_seq, (256, 128))
    nq = t_seq // qb
    kblk_per_q = qb // WINDOW
    n_kblk = kblk_per_q + 2
    nkb_seq = t_seq // WINDOW
    kvw = k.shape[1]

    def win_spec(j):
        def idx(b, i):
            blk = jnp.clip(i * kblk_per_q + (j - 1), 0, nkb_seq - 1)
            return (b * nkb_seq + blk, 0)
        return pl.BlockSpec((WINDOW, kvw), idx)

    ctx_spec = pl.BlockSpec((t_ctx, kvw), lambda b, i: (b, 0))
    in_specs = ([pl.BlockSpec((qb, ATT_WIDTH), lambda b, i: (b * nq + i, 0)), ctx_spec, ctx_spec]
                + [win_spec(j) for j in range(n_kblk)] * 2
                + [pl.BlockSpec(memory_space=pltpu.SMEM)])
    return pl.pallas_call(
        functools.partial(_window_attn_kernel, n_kblk=n_kblk, t_seq=t_seq),
        grid=(bsz, nq),
        in_specs=in_specs,
        out_specs=pl.BlockSpec((qb, ATT_WIDTH), lambda b, i: (b * nq + i, 0)),
        out_shape=jax.ShapeDtypeStruct((m, ATT_WIDTH), BF16),
        compiler_params=_cparams(2),
        name="window_attn",
    )(q, k_ctx, v_ctx, *([k] * n_kblk), *([v] * n_kblk), sink)


def _dense_attn_kernel(*refs, n_kv_src, has_sink):
    q_ref = refs[0]
    k_refs = refs[1:1 + n_kv_src]
    v_refs = refs[1 + n_kv_src:1 + 2 * n_kv_src]
    rest = refs[1 + 2 * n_kv_src:]
    sink_ref = rest[0] if has_sink else None
    o_ref = rest[-1]
    if n_kv_src == 1:
        k, v = k_refs[0][...], v_refs[0][...]
    else:
        k = jnp.concatenate([r[...] for r in k_refs], axis=0)
        v = jnp.concatenate([r[...] for r in v_refs], axis=0)
    _attend(q_ref, slice(0, q_ref.shape[0]), k, v, None, sink_ref, o_ref)


def _dense_attention(q, ks, vs, sink, bsz, t_q):
    m = q.shape[0]
    qb = _pick_tile(t_q, (512, 256, 128))
    nq = t_q // qb
    kv_specs = [pl.BlockSpec((a.shape[0] // bsz, a.shape[1]), lambda b, i: (b, 0)) for a in ks]
    in_specs = [pl.BlockSpec((qb, ATT_WIDTH), lambda b, i: (b * nq + i, 0))] + kv_specs * 2
    args = [q, *ks, *vs]
    if sink is not None:
        in_specs.append(pl.BlockSpec(memory_space=pltpu.SMEM))
        args.append(sink)
    return pl.pallas_call(
        functools.partial(_dense_attn_kernel, n_kv_src=len(ks), has_sink=sink is not None),
        grid=(bsz, nq),
        in_specs=in_specs,
        out_specs=pl.BlockSpec((qb, ATT_WIDTH), lambda b, i: (b * nq + i, 0)),
        out_shape=jax.ShapeDtypeStruct((m, ATT_WIDTH), BF16),
        compiler_params=_cparams(2),
        name="dense_attn",
    )(*args)


def _split3(x):
    hi = x.astype(BF16)
    r1 = x - hi.astype(F32)
    mid = r1.astype(BF16)
    lo = (r1 - mid.astype(F32)).astype(BF16)
    return hi, mid, lo


GATE_QUANTITIES = 5
GQ_A, GQ_INTER, GQ_NRM, GQ_END, GQ_DECAY = range(GATE_QUANTITIES)


def _mlstm_gate_kernel(gt_ref, gbt_ref, m0_ref, o_ref, rpc_ref, m_out, m_scr, *, rev):
    gi = pl.program_id(0)
    n_g = pl.num_programs(0)
    bsz, _, blk = gt_ref.shape
    rows = bsz * N_GATES
    n_chunks = blk // CHUNK

    @pl.when(gi == 0)
    def _():
        m_scr[...] = m0_ref[...]

    gates_t = (gt_ref[...] + gbt_ref[...]).reshape(rows, blk)
    r = lax.broadcasted_iota(jnp.int32, (blk, blk), 0)
    c = lax.broadcasted_iota(jnp.int32, (blk, blk), 1)
    feeds_b = jnp.where((r >= c) if rev else (r <= c), 1.0, 0.0).astype(BF16)
    brow = sum(_mm(piece, feeds_b) for piece in _split3(_log_sigmoid(gates_t)))
    b_all = pltpu.roll(brow, rows - C_HEADS, 0)
    rp = gates_t - b_all
    lane = lax.broadcasted_iota(jnp.int32, (rows, blk), 1)
    pos = lane % CHUNK
    seg = rp
    shift = 1
    while shift < CHUNK:
        if rev:
            seg = jnp.where(pos < CHUNK - shift, jnp.maximum(seg, pltpu.roll(seg, blk - shift, 1)), seg)
        else:
            seg = jnp.where(pos >= shift, jnp.maximum(seg, pltpu.roll(seg, shift, 1)), seg)
        shift *= 2
    m_init = m_scr[:, 0:1]
    m_prev = m_init
    b_off = jnp.zeros((rows, 1), F32)
    m_lane = jnp.zeros((rows, blk), F32)
    off_lane = jnp.zeros((rows, blk), F32)
    for ch in (range(n_chunks - 1, -1, -1) if rev else range(n_chunks)):
        end = ch * CHUNK if rev else (ch + 1) * CHUNK - 1
        here = (lane // CHUNK) == ch
        m_lane = jnp.where(here, m_prev, m_lane)
        off_lane = jnp.where(here, b_off, off_lane)
        b_end = b_all[:, end:end + 1]
        m_prev = (b_end - b_off) + jnp.maximum(m_prev, b_off + seg[:, end:end + 1])
        b_off = b_end
    m_last, b_last = m_prev, b_off
    a = off_lane - jnp.maximum(m_lane, off_lane + seg)
    quantities = [None] * GATE_QUANTITIES
    quantities[GQ_A] = a
    quantities[GQ_INTER] = jnp.exp(a + m_init)
    quantities[GQ_NRM] = jnp.exp(a - b_all)
    quantities[GQ_END] = jnp.exp(rp + (b_last - m_last))
    quantities[GQ_DECAY] = jnp.broadcast_to(jnp.exp(b_last + m_init - m_last), (rows, blk))
    o_ref[...] = jnp.concatenate([x.reshape(bsz, N_GATES, blk) for x in quantities], axis=1)
    rp_t = rp.T
    for b in range(bsz):
        rpc_ref[b] = rp_t[:, b * N_GATES:(b + 1) * N_GATES]
    m_scr[...] = jnp.broadcast_to(m_last, (rows, LANES))

    @pl.when(gi == n_g - 1)
    def _():
        m_out[...] = m_scr[...]


def _mlstm_gates(gt, gate_b, m_state, t_seq, rev):
    bsz = gt.shape[0]
    blk = _pick_tile(t_seq, (256, 128))
    ng = t_seq // blk
    step = (lambda i: ng - 1 - i) if rev else (lambda i: i)
    tok = lambda i: (0, 0, step(i))
    return pl.pallas_call(
        functools.partial(_mlstm_gate_kernel, rev=rev),
        grid=(ng,),
        in_specs=[pl.BlockSpec((bsz, N_GATES, blk), tok), _resident((1, N_GATES, 1)), _resident(m_state.shape)],
        out_specs=[pl.BlockSpec((bsz, GATE_QUANTITIES * N_GATES, blk), tok),
                   pl.BlockSpec((bsz, blk, N_GATES), lambda i: (0, step(i), 0)),
                   _resident(m_state.shape)],
        out_shape=[jax.ShapeDtypeStruct((bsz, GATE_QUANTITIES * N_GATES, t_seq), F32),
                   jax.ShapeDtypeStruct((bsz, t_seq, N_GATES), F32),
                   jax.ShapeDtypeStruct(m_state.shape, F32)],
        scratch_shapes=[pltpu.VMEM(m_state.shape, F32)],
        compiler_params=_cparams(1),
        name="mlstm_gates_bwd" if rev else "mlstm_gates_fwd",
    )(gt, gate_b.reshape(1, N_GATES, 1), m_state)


N_DIR = 2


def _mlstm_kernel(*refs):
    per_dir = [refs[0:6], refs[6:12]]
    h_refs, s_outs, s_scr = refs[12:14], refs[14:16], refs[16]
    gi = pl.program_id(1)
    n_g = pl.num_programs(1)
    blk = refs[0].shape[0]

    @pl.when(gi == 0)
    def _():
        for d in range(N_DIR):
            s_scr[d] = per_dir[d][5][0]

    r = lax.broadcasted_iota(jnp.int32, (blk, blk), 0)
    c = lax.broadcasted_iota(jnp.int32, (blk, blk), 1)
    ones = jnp.ones((C_HEAD_DIM, blk), BF16)
    units = []
    for d, rev in enumerate((False, True)):
        k_ref, qt_ref, vt_ref, rpc_ref, gq_ref, _ = per_dir[d]
        feeds = (r >= c) if rev else (r <= c)
        base = 2 * C_HEADS if rev else 0
        rp_col = rpc_ref[0][:, base:base + C_HEADS]
        gq = gq_ref[0]
        for hd in range(C_HEADS):
            lanes = slice(hd * C_HEAD_DIM, (hd + 1) * C_HEAD_DIM)
            rows = [gq[q * N_GATES + base + hd:q * N_GATES + base + hd + 1] for q in range(GATE_QUANTITIES)]
            units.append(dict(
                d=d, hd=hd, lanes=lanes, feeds=feeds, rows=rows, rp=rp_col[:, hd:hd + 1],
                kh=k_ref[:, lanes],
                qth=qt_ref[lanes, :],
                v_aug=jnp.concatenate([vt_ref[lanes, :], ones], axis=0),
                st=s_scr[d, hd]))
    for u in units:
        u["s"] = _mm(u["kh"], u["qth"])
        u["inter"] = _mm(u["st"].astype(BF16), u["qth"])
        x_t = (u["rows"][GQ_END] * u["v_aug"].astype(F32)).astype(BF16)
        u["st_new"] = u["rows"][GQ_DECAY][:, :C_HEAD_DIM] * u["st"] + _mm(x_t, u["kh"])
    for u in units:
        w_t = jnp.exp(jnp.where(u["feeds"], u["rp"] + u["rows"][GQ_A], -jnp.inf))
        u["p"] = (u["s"] * w_t).astype(BF16)
    for u in units:
        tot = _mm(u["v_aug"], u["p"]) + u["rows"][GQ_INTER] * u["inter"]
        u["h_t"] = tot[:C_HEAD_DIM] / jnp.maximum(jnp.abs(tot[C_HEAD_DIM:]), u["rows"][GQ_NRM])
    for u in units:
        h_refs[u["d"]][:, u["lanes"]] = u["h_t"].T
        s_scr[u["d"], u["hd"]] = u["st_new"]

    @pl.when(gi == n_g - 1)
    def _():
        for d in range(N_DIR):
            s_outs[d][0] = s_scr[d]


def _mlstm_scan(k, qt, vt, rpcs, gqs, states, bsz, t_seq):
    m = k.shape[0]
    blk = _pick_tile(t_seq, (256, 128))
    ng = t_seq // blk
    st_spec = pl.BlockSpec((1, C_HEADS, 2 * C_HEAD_DIM, C_HEAD_DIM), lambda b, i: (b, 0, 0, 0))
    in_specs, args, h_specs = [], [], []
    for d, step in enumerate((lambda i: i, lambda i: ng - 1 - i)):
        tok = lambda b, i, step=step: (b * ng + step(i), 0)
        tok_t = lambda b, i, step=step: (0, b * ng + step(i))
        in_specs += [
            pl.BlockSpec((blk, C_WIDTH), tok),
            pl.BlockSpec((C_WIDTH, blk), tok_t),
            pl.BlockSpec((C_WIDTH, blk), tok_t),
            pl.BlockSpec((1, blk, N_GATES), lambda b, i, step=step: (b, step(i), 0)),
            pl.BlockSpec((1, GATE_QUANTITIES * N_GATES, blk), lambda b, i, step=step: (b, 0, step(i))),
            st_spec,
        ]
        args += [k, qt, vt, rpcs[d], gqs[d], states[d]]
        h_specs.append(pl.BlockSpec((blk, C_WIDTH), tok))
    return pl.pallas_call(
        _mlstm_kernel,
        grid=(bsz, ng),
        in_specs=in_specs,
        out_specs=h_specs + [st_spec] * N_DIR,
        out_shape=[jax.ShapeDtypeStruct((m, C_WIDTH), F32)] * N_DIR
                  + [jax.ShapeDtypeStruct(states[0].shape, F32)] * N_DIR,
        scratch_shapes=[pltpu.VMEM((N_DIR, C_HEADS, 2 * C_HEAD_DIM, C_HEAD_DIM), F32)],
        compiler_params=_cparams(2),
        name="mlstm_scan",
    )(*args)


MERGE_COL_CHUNK = 256


def _merge_kernel(x_ref, mod_ref, ng_ref, ya_ref, yb_ref, hf_ref, hb_ref, om_ref, hg_ref,
                  w_gl, w_br, w_out, o_ref, acc_scr):
    x = x_ref[...]
    mod = mod_ref[0]
    ng = ng_ref[...]
    h = _mod_norm(x, mod, ng, 1).astype(BF16)
    d = x.shape[1]
    hsum = hf_ref[...] + hb_ref[...]
    om = om_ref[...]
    ym = []
    for j in range(C_HEADS):
        lanes = slice(j * C_HEAD_DIM, (j + 1) * C_HEAD_DIM)
        ym.append((_sigmoid(om[:, lanes]) * _rms(hsum[:, lanes], hg_ref[:, lanes])).astype(BF16))
    ys = [ya_ref[...], yb_ref[...], jnp.concatenate(ym, axis=1)]
    n_chunks = d // MERGE_COL_CHUNK

    def products(c):
        cols = slice(c * MERGE_COL_CHUNK, (c + 1) * MERGE_COL_CHUNK)
        return [(_mm(h, w_gl[:, n * d + cols.start:n * d + cols.stop]), _mm(ys[n], w_br[n, :, cols]))
                for n in range(N_BRANCH)]

    nxt = products(0)
    for c in range(n_chunks):
        cur = nxt
        if c + 1 < n_chunks:
            nxt = products(c + 1)
        acc = sum(_sigmoid(gl) * z for gl, z in cur)
        acc_scr[:, c * MERGE_COL_CHUNK:(c + 1) * MERGE_COL_CHUNK] = acc.astype(BF16)
    y = _mm(acc_scr[...], w_out[...])
    o_ref[...] = _gated_residual(x, y, mod, ng, 1, 1.0)


def _merge(x, mod, ng, ya, yb, hf, hb, om, head_g, w_gl, w_br, w_out, rows_per_mod, mod_base):
    m, d = x.shape
    tm = _pick_tile(rows_per_mod)
    tiles_per_mod = rows_per_mod // tm
    row = lambda i: (i, 0)
    return pl.pallas_call(
        _merge_kernel,
        grid=(m // tm,),
        in_specs=[
            pl.BlockSpec((tm, d), row),
            _mod_spec(mod, lambda i: mod_base + i // tiles_per_mod),
            _resident_slab(ng),
            pl.BlockSpec((tm, ATT_WIDTH), row),
            pl.BlockSpec((tm, ATT_WIDTH), row),
            pl.BlockSpec((tm, C_WIDTH), row),
            pl.BlockSpec((tm, C_WIDTH), row),
            pl.BlockSpec((tm, C_WIDTH), row),
            _resident_slab(head_g),
            _resident_slab(w_gl),
            _resident_slab(w_br),
            _resident_slab(w_out),
        ],
        out_specs=pl.BlockSpec((tm, d), row),
        out_shape=jax.ShapeDtypeStruct((m, d), F32),
        scratch_shapes=[pltpu.VMEM((tm, d), BF16)],
        compiler_params=_cparams(1),
        name="merge",
    )(x, mod.arr, ng.arr, ya, yb, hf, hb, om, head_g.arr, w_gl.arr, w_br.arr, w_out.arr)


def _rope_tables(t_seq):
    rows = t_seq // GRID_W
    row = jnp.repeat(jnp.arange(rows), GRID_W)
    col = jnp.tile(jnp.arange(GRID_W), rows)
    freqs = ROPE_THETA ** (-jnp.arange(ROPE_PAIRS, dtype=F32) / ROPE_PAIRS)
    ang = jnp.stack([row[:, None] * freqs, col[:, None] * freqs], axis=1)
    cos, sin = jnp.cos(ang), jnp.sin(ang)
    cos_head = jnp.concatenate([cos[:, 0], cos[:, 0], cos[:, 1], cos[:, 1]], axis=-1)
    sin_head = jnp.concatenate([-sin[:, 0], sin[:, 0], -sin[:, 1], sin[:, 1]], axis=-1)
    return jnp.tile(cos_head, (1, 2)), jnp.tile(sin_head, (1, 2))


def _split_in_weights(w_in):
    sizes = (ATT_WIDTH, N_KV * HEAD_DIM, N_KV * HEAD_DIM, ATT_WIDTH, N_KV * HEAD_DIM, N_KV * HEAD_DIM,
             C_WIDTH, C_WIDTH, C_WIDTH, C_WIDTH, N_GATES, N_BRANCH * w_in.shape[1])
    parts, start = [], 0
    for s in sizes:
        parts.append(w_in[..., start:start + s])
        start += s
    qa, ka, va, qb, kb, vb, qm, km, vm, om, gm, gl = parts
    cat = lambda ws: jnp.concatenate(ws, axis=-1)

    def doubled(w):
        return cat([w[..., :HEAD_DIM], w[..., :HEAD_DIM], w[..., HEAD_DIM:], w[..., HEAD_DIM:]])

    bf = lambda w: w.astype(BF16)
    weights = [bf(qa), bf(cat([doubled(ka), doubled(va)])), bf(qb), bf(cat([doubled(kb), doubled(vb)])),
               bf(cat([qm, km])), bf(jnp.swapaxes(cat([vm, gm]), -1, -2)), bf(om)]
    return weights, bf(gl)


def kernel(x, c, ctx, c_ctx, w_ada, b_ada, norm_g, ffn_w_gate, ffn_w_up, ffn_w_down, w_in, attn_sink,
           qk_norm_g, conv_w, conv_b, mlstm_gate_b, mlstm_norm_g, w_branch, w_out):
    bsz, t_lat, d = x.shape
    t_ctx = ctx.shape[1]
    depth = w_ada.shape[0]
    xs = x.reshape(bsz * t_lat, d)
    cs = ctx.reshape(bsz * t_ctx, d)
    mods = _modulation(jnp.concatenate([c, c_ctx[None]], axis=0), w_ada, b_ada)
    rope_tabs = _rope_tables(t_lat)
    zero_state = jnp.zeros((bsz, C_HEADS, 2 * C_HEAD_DIM, C_HEAD_DIM), F32)
    zero_m = jnp.zeros((bsz * N_GATES, LANES), F32)

    ffn_all = [w.astype(BF16) for w in (ffn_w_gate, ffn_w_up, ffn_w_down)]
    in_w_all, w_gl_all = _split_in_weights(w_in)
    w_br_all, w_o_all = w_branch.astype(BF16), w_out.astype(BF16)
    qkn_all = jnp.tile(qk_norm_g, (1, 1, 2))
    conv_b_all = conv_b[:, None, :]
    head_g_all = mlstm_norm_g[:, None, :]

    for l in range(depth):
        last = l == depth - 1
        mod, ng = _Slab(mods, l), _Slab(norm_g, l)
        ffn = [[_Slab(w, l, k) for w in ffn_all] for k in range(2)]
        in_w = [_Slab(w, l) for w in in_w_all]
        w_gl, w_br, w_o = _Slab(w_gl_all, l), _Slab(w_br_all, l), _Slab(w_o_all, l)
        qkn, cw, cb, head_g = _Slab(qkn_all, l), _Slab(conv_w, l), _Slab(conv_b_all, l), _Slab(head_g_all, l)
        sink = attn_sink[l].reshape(1, N_HEADS)
        lat = dict(rows_per_mod=t_lat, mod_base=0)
        con = dict(rows_per_mod=bsz * t_ctx, mod_base=bsz)

        xs = _half_ffn(xs, mod, ng, *ffn[0], 0, **lat)
        cs = _half_ffn(cs, mod, ng, *ffn[0], 0, **con)

        (qa, ka, va, qb, kb, vb, qmt, km, vm, om, gmt) = _in_projection(
            xs, mod, ng, rope_tabs, qkn, in_w, cw, cb, t_seq=t_lat, **lat)
        (qa_c, ka_c, va_c, qb_c, kb_c, vb_c, qmt_c, km_c, vm_c, om_c, gmt_c) = _in_projection(
            cs, mod, ng, None, qkn, in_w, cw, cb, t_seq=t_ctx, **con)

        ya = _window_attention(qa, ka, va, ka_c, va_c, sink, bsz, t_lat, t_ctx)
        yb = _dense_attention(qb, [kb_c, kb], [vb_c, vb], None, bsz, t_lat)

        gb = mlstm_gate_b[l]
        gq_cf, rp_cf, m_f = _mlstm_gates(gmt_c, gb, zero_m, t_ctx, False)
        gq_cb, rp_cb, m_b = _mlstm_gates(gmt_c, gb, zero_m, t_ctx, True)
        gq_f, rp_f, _ = _mlstm_gates(gmt, gb, m_f, t_lat, False)
        gq_b, rp_b, _ = _mlstm_gates(gmt, gb, m_b, t_lat, True)
        hf_c, hb_c, st_f, st_b = _mlstm_scan(km_c, qmt_c, vm_c, (rp_cf, rp_cb), (gq_cf, gq_cb),
                                             (zero_state, zero_state), bsz, t_ctx)
        hf, hb, _, _ = _mlstm_scan(km, qmt, vm, (rp_f, rp_b), (gq_f, gq_b), (st_f, st_b), bsz, t_lat)

        xs = _merge(xs, mod, ng, ya, yb, hf, hb, om, head_g, w_gl, w_br, w_o, **lat)
        xs = _half_ffn(xs, mod, ng, *ffn[1], 2, **lat)
        if not last:
            ya_c = _dense_attention(qa_c, [ka_c], [va_c], sink, bsz, t_ctx)
            yb_c = _dense_attention(qb_c, [kb_c], [vb_c], None, bsz, t_ctx)
            cs = _merge(cs, mod, ng, ya_c, yb_c, hf_c, hb_c, om_c, head_g, w_gl, w_br, w_o, **con)
            cs = _half_ffn(cs, mod, ng, *ffn[1], 2, **con)
    return xs.reshape(bsz, t_lat, d)
```
